```python
import math
import jax
import jax.numpy as jnp
from jax import lax
import numpy as np

D_MODEL = 1024
BATCH = 8
SEQ = 2048
DEPTH = 4
DEC_BATCH = 128
DEC_SEQ = 8
PAST_LEN = 8192
PAGE_SIZE = 128

N_MIXERS = 4
EPS = 1e-6
NEG = -1e30
BLK = 128
HEAD_DIM = 64

GLA_HEADS = 4
GLA_DK = D_MODEL // 2 // GLA_HEADS
GLA_DV = D_MODEL // GLA_HEADS
GLA_RANK = 16
GLA_NORMALIZER = 16.0
GLA_CHUNK = 64

SWA_HEADS = D_MODEL // HEAD_DIM
SWA_KV_HEADS = 4
SWA_GROUP = SWA_HEADS // SWA_KV_HEADS
SWA_WINDOW = 128

DIL_GROUPS = ((128, 1), (512, 4), (2048, 16))
DIL_HEADS = 8

DIFF_HEADS = D_MODEL // (2 * HEAD_DIM)
DIFF_KV_HEADS = 4
DIFF_GROUP = DIFF_HEADS // DIFF_KV_HEADS
DIFF_LAYER_IDX = 3
LAMBDA_INIT = 0.8 - 0.6 * math.exp(-0.3 * DIFF_LAYER_IDX)

D_FF = ((8 * D_MODEL // 3 + 127) // 128) * 128
N_EXPERTS = 8
TOP_K = 2
D_EXPERT = 7 * D_MODEL // 2

F32 = jnp.float32

kernel_name = 'hybrid_gla_swa_dilated_diff_decoder_step'


def rms(x, g):
    xf = x.astype(F32)
    y = xf * lax.rsqrt(jnp.mean(xf * xf, axis=-1, keepdims=True) + EPS)
    return (y * g.astype(F32)).astype(x.dtype)


def ada_mod(c, w, b):
    m = jax.nn.silu(c) @ w + b
    return jnp.split(m[:, None, :], 6, axis=-1)


def adaln(x, g, shift, scale):
    xf = x.astype(F32)
    y = xf * lax.rsqrt(jnp.mean(xf * xf, axis=-1, keepdims=True) + EPS)
    return (y * g.astype(F32) * (1.0 + scale.astype(F32)) + shift.astype(F32)).astype(x.dtype)


def softmax_stats(s, sink=None):
    m = jnp.max(s, axis=-1)
    if sink is not None:
        m = jnp.maximum(m, sink)
    p = jnp.exp(s - m[..., None])
    l = jnp.sum(p, axis=-1)
    if sink is not None:
        l = l + jnp.exp(sink - m)
    return p, m, l


def gla_scan(q, k, v, log_a, s0):
    bsz, L, H, _ = q.shape
    dv = v.shape[-1]
    C = math.gcd(L, GLA_CHUNK)
    n = L // C
    r = lambda t: t.reshape(bsz, n, C, H, t.shape[-1])
    q, k, v, log_a = r(q), r(k), r(v), r(log_a)
    b = jnp.cumsum(log_a, axis=2)
    b_last = b[:, :, -1:]
    q_dec = q * jnp.exp(b)
    k_inv = k * jnp.exp(-b)
    k_dec = k * jnp.exp(b_last - b)
    causal = jnp.tril(jnp.ones((C, C), bool))
    att = jnp.where(causal, jnp.einsum('bnchd,bnshd->bnhcs', q_dec, k_inv), 0.0)
    o_intra = jnp.einsum('bnhcs,bnshe->bnche', att, v)
    decay = jnp.exp(b_last[:, :, 0])

    def step(S, xs):
        qc, kc, vc, dc = xs
        o = jnp.einsum('bchd,bhde->bche', qc, S)
        S = dc[..., None] * S + jnp.einsum('bchd,bche->bhde', kc, vc)
        return S, o

    mv = lambda t: jnp.moveaxis(t, 1, 0)
    s_fin, o_inter = lax.scan(step, s0, (mv(q_dec), mv(k_dec), mv(v), mv(decay)))
    o = o_intra + jnp.moveaxis(o_inter, 0, 1)
    return o.reshape(bsz, L, H, dv), s_fin


def gla_mixer(h, s0, w_in, w_gate2, b_gate, onorm_g, w_out):
    bsz, L, _ = h.shape
    nq = GLA_HEADS * GLA_DK
    nv = GLA_HEADS * GLA_DV
    q, k, v, g, a_lr = jnp.split(h @ w_in, [nq, 2 * nq, 2 * nq + nv, 2 * nq + 2 * nv], axis=-1)
    q = q.reshape(bsz, L, GLA_HEADS, GLA_DK).astype(F32) * GLA_DK ** -0.5
    k = k.reshape(bsz, L, GLA_HEADS, GLA_DK).astype(F32)
    v = v.reshape(bsz, L, GLA_HEADS, GLA_DV).astype(F32)
    log_a = jax.nn.log_sigmoid((a_lr @ w_gate2 + b_gate).astype(F32)) / GLA_NORMALIZER
    log_a = log_a.reshape(bsz, L, GLA_HEADS, GLA_DK)
    o, s_fin = gla_scan(q, k, v, log_a, s0.astype(F32))
    o = rms(o, onorm_g).reshape(bsz, L, nv) * jax.nn.silu(g.astype(F32))
    return o.astype(h.dtype) @ w_out, s_fin.astype(s0.dtype)


def banded_attn(q, k, v, window, sink=None):
    n, L, hk, g, dh = q.shape
    nb = L // BLK
    qb = q.reshape(n, nb, BLK, hk, g, dh).astype(F32)

    def band(t):
        tb = t.reshape(n, nb, BLK, hk, t.shape[-1]).astype(F32)
        prev = jnp.concatenate([jnp.zeros_like(tb[:, :1]), tb[:, :-1]], axis=1)
        return jnp.concatenate([prev, tb], axis=2)

    kb, vb = band(k), band(v)
    s = jnp.einsum('bnqhgd,bnkhd->bnhgqk', qb, kb) * dh ** -0.5
    blk = jnp.arange(nb)[:, None, None] * BLK
    qpos = blk + jnp.arange(BLK)[None, :, None]
    kpos = blk - BLK + jnp.arange(2 * BLK)[None, None, :]
    dist = qpos - kpos
    valid = (dist >= 0) & (dist <= window) & (kpos >= 0)
    s = jnp.where(valid[:, None, None], s, NEG)
    p, m, l = softmax_stats(s, None if sink is None else sink[:, :, None])
    o = jnp.einsum('bnhgqk,bnkhd->bnqhgd', p, vb) / jnp.transpose(l, (0, 1, 4, 2, 3))[..., None]
    lse = jnp.transpose(m + jnp.log(l), (0, 1, 4, 2, 3))
    return o.reshape(n, L, hk, g, dh), lse.reshape(n, L, hk, g)


def swa_project(h, w_in, qn, kn):
    bsz, L, _ = h.shape
    q, k, v = jnp.split(h @ w_in, [SWA_HEADS * HEAD_DIM, (SWA_HEADS + SWA_KV_HEADS) * HEAD_DIM], axis=-1)
    q = rms(q.reshape(bsz, L, SWA_KV_HEADS, SWA_GROUP, HEAD_DIM), qn)
    k = rms(k.reshape(bsz, L, SWA_KV_HEADS, HEAD_DIM), kn)
    v = v.reshape(bsz, L, SWA_KV_HEADS, HEAD_DIM)
    return q, k, v


def swa_prompt(h, w_in, qn, kn, sinks, w_out):
    bsz, L, _ = h.shape
    q, k, v = swa_project(h, w_in, qn, kn)
    sink = sinks.astype(F32).reshape(SWA_KV_HEADS, SWA_GROUP)
    o, _ = banded_attn(q, k, v, SWA_WINDOW, sink)
    keep = min(SWA_WINDOW, L)
    buf = jnp.stack([k[:, L - keep:], v[:, L - keep:]], axis=2)
    return o.reshape(bsz, L, -1).astype(h.dtype) @ w_out, buf


def swa_sample(h, buf, w_in, qn, kn, sinks, w_out):
    bsz, S, _ = h.shape
    lb = buf.shape[1]
    q, k, v = swa_project(h, w_in, qn, kn)
    sink = sinks.astype(F32).reshape(SWA_KV_HEADS, SWA_GROUP)
    kv = jnp.concatenate([buf, jnp.stack([k, v], axis=2).astype(buf.dtype)], axis=1)
    kf = kv[:, :, 0].astype(F32)
    vf = kv[:, :, 1].astype(F32)
    s = jnp.einsum('bqhgd,bkhd->bhgqk', q.astype(F32), kf) * HEAD_DIM ** -0.5
    dist = lb + jnp.arange(S)[:, None] - jnp.arange(lb + S)[None, :]
    s = jnp.where((dist >= 0) & (dist <= SWA_WINDOW), s, NEG)
    p, m, l = softmax_stats(s, sink[:, :, None])
    o = jnp.einsum('bhgqk,bkhd->bqhgd', p, vf) / jnp.transpose(l, (0, 3, 1, 2))[..., None]
    return o.reshape(bsz, S, -1).astype(h.dtype) @ w_out, kv[:, S:]


def dil_project(h, w_in, qn, kn):
    bsz, L, _ = h.shape
    p = (h @ w_in).reshape(bsz, L, len(DIL_GROUPS), 3, DIL_HEADS, HEAD_DIM)
    return rms(p[:, :, :, 0], qn), rms(p[:, :, :, 1], kn), p[:, :, :, 2]


def combine_groups(outs, lses):
    w = jax.nn.softmax(jnp.stack(lses, axis=0), axis=0)
    return jnp.sum(w[..., None] * jnp.stack(outs, axis=0), axis=0)


def dil_group_prompt(q, k, v, window, dil):
    bsz, L, H, dh = q.shape
    lr = L // dil
    lp = -(-lr // BLK) * BLK

    def to_res(t):
        t = t.reshape(bsz, lr, dil, H, dh).transpose(0, 2, 1, 3, 4).reshape(bsz * dil, lr, H, dh)
        return jnp.pad(t, ((0, 0), (0, lp - lr), (0, 0), (0, 0)))

    o, lse = banded_attn(to_res(q)[:, :, :, None], to_res(k), to_res(v), window // dil)
    o = o[:, :lr, :, 0].reshape(bsz, dil, lr, H, dh).transpose(0, 2, 1, 3, 4).reshape(bsz, L, H, dh)
    lse = lse[:, :lr, :, 0].reshape(bsz, dil, lr, H).transpose(0, 2, 1, 3).reshape(bsz, L, H)
    return o, lse


def dil_prompt(h, w_in, qn, kn, w_out):
    bsz, L, _ = h.shape
    q, k, v = dil_project(h, w_in, qn, kn)
    outs, lses, bufs = [], [], []
    for gi, (win, dil) in enumerate(DIL_GROUPS):
        o, lse = dil_group_prompt(q[:, :, gi], k[:, :, gi], v[:, :, gi], win, dil)
        outs.append(o)
        lses.append(lse)
        keep = min(win, L)
        bufs.append(jnp.stack([k[:, L - keep:, gi], v[:, L - keep:, gi]], axis=2))
    o = combine_groups(outs, lses)
    return o.reshape(bsz, L, -1).astype(h.dtype) @ w_out, tuple(bufs)


def dil_group_sample(q, k, v, buf, window, dil):
    S = q.shape[1]
    dh = q.shape[-1]
    lb = buf.shape[1]
    kv = jnp.concatenate([buf, jnp.stack([k, v], axis=2).astype(buf.dtype)], axis=1)
    n_taps = window // dil + 1
    idx = lb + jnp.arange(S)[:, None] - dil * jnp.arange(n_taps)[None, :]
    valid = idx >= 0
    g = kv[:, jnp.maximum(idx, 0)].astype(F32)
    s = jnp.einsum('bqhd,bqthd->bhqt', q.astype(F32), g[:, :, :, 0]) * dh ** -0.5
    s = jnp.where(valid, s, NEG)
    p, m, l = softmax_stats(s)
    o = jnp.einsum('bhqt,bqthd->bqhd', p, g[:, :, :, 1]) / jnp.transpose(l, (0, 2, 1))[..., None]
    lse = jnp.transpose(m + jnp.log(l), (0, 2, 1))
    return o, lse, kv[:, S:]


def dil_sample(h, bufs, w_in, qn, kn, w_out):
    bsz, S, _ = h.shape
    q, k, v = dil_project(h, w_in, qn, kn)
    outs, lses, new_bufs = [], [], []
    for gi, (win, dil) in enumerate(DIL_GROUPS):
        o, lse, nbuf = dil_group_sample(q[:, :, gi], k[:, :, gi], v[:, :, gi], bufs[gi], win, dil)
        outs.append(o)
        lses.append(lse)
        new_bufs.append(nbuf)
    o = combine_groups(outs, lses)
    return o.reshape(bsz, S, -1).astype(h.dtype) @ w_out, tuple(new_bufs)


def diff_project(h, w_in, qn, kn):
    bsz, L, _ = h.shape
    nq = 2 * DIFF_HEADS * HEAD_DIM
    nk = 2 * DIFF_KV_HEADS * HEAD_DIM
    q, k, v = jnp.split(h @ w_in, [nq, nq + nk], axis=-1)
    q = rms(q.reshape(bsz, L, DIFF_KV_HEADS, DIFF_GROUP, 2, HEAD_DIM), qn)
    k = rms(k.reshape(bsz, L, DIFF_KV_HEADS, 2, HEAD_DIM), kn)
    v = v.reshape(bsz, L, DIFF_KV_HEADS, 2 * HEAD_DIM)
    return q, k, v


def diff_finish(o, lam_vec, subln_g, w_out, dtype):
    lam_vec = lam_vec.astype(F32)
    lam = jnp.exp(jnp.sum(lam_vec[0] * lam_vec[1])) - jnp.exp(jnp.sum(lam_vec[2] * lam_vec[3])) + LAMBDA_INIT
    o = o[..., 0, :] - lam * o[..., 1, :]
    o = rms(o, subln_g) * (1.0 - LAMBDA_INIT)
    bsz, L = o.shape[:2]
    return o.reshape(bsz, L, -1).astype(dtype) @ w_out


def diff_prompt(h, w_in, qn, kn, lam_vec, subln_g, w_out):
    bsz, L, _ = h.shape
    q, k, v = diff_project(h, w_in, qn, kn)
    nb = L // BLK
    kf = k.astype(F32)
    vf = v.astype(F32)
    qb = jnp.moveaxis(q.astype(F32).reshape(bsz, nb, BLK, DIFF_KV_HEADS, DIFF_GROUP, 2, HEAD_DIM), 1, 0)
    kpos = jnp.arange(L)

    def block(args):
        qi, bi = args
        s = jnp.einsum('bqhgmd,bkhmd->bhgmqk', qi, kf) * HEAD_DIM ** -0.5
        qpos = bi * BLK + jnp.arange(BLK)
        s = jnp.where(kpos[None, :] <= qpos[:, None], s, NEG)
        p = jax.nn.softmax(s, axis=-1)
        return jnp.einsum('bhgmqk,bkhe->bqhgme', p, vf)

    o = lax.map(block, (qb, jnp.arange(nb)))
    o = jnp.moveaxis(o, 0, 1).reshape(bsz, L, DIFF_KV_HEADS, DIFF_GROUP, 2, 2 * HEAD_DIM)
    y = diff_finish(o, lam_vec, subln_g, w_out, h.dtype)
    rows = jnp.stack([k.reshape(bsz, L, DIFF_KV_HEADS, 2 * HEAD_DIM), v], axis=2)
    return y, rows


def diff_sample(h, cache_kv, page_table, w_in, qn, kn, lam_vec, subln_g, w_out):
    bsz, S, _ = h.shape
    q, k, v = diff_project(h, w_in, qn, kn)
    qf = q.astype(F32) * HEAD_DIM ** -0.5

    def attend(carry, kk, vv, valid):
        m, l, acc = carry
        s = jnp.einsum('bqhgmd,bkhmd->bhgmqk', qf, kk)
        if valid is not None:
            s = jnp.where(valid, s, NEG)
        m_new = jnp.maximum(m, jnp.max(s, axis=-1))
        corr = jnp.exp(m - m_new)
        p = jnp.exp(s - m_new[..., None])
        l = l * corr + jnp.sum(p, axis=-1)
        acc = acc * corr[..., None] + jnp.einsum('bhgmqk,bkhe->bhgmqe', p, vv)
        return (m_new, l, acc)

    def page_step(carry, pt_col):
        kv = cache_kv[pt_col].astype(F32)
        kk = kv[:, :, 0].reshape(bsz, PAGE_SIZE, DIFF_KV_HEADS, 2, HEAD_DIM)
        return attend(carry, kk, kv[:, :, 1], None), None

    stat_shape = (bsz, DIFF_KV_HEADS, DIFF_GROUP, 2, S)
    init = (jnp.full(stat_shape, NEG, F32), jnp.zeros(stat_shape, F32),
            jnp.zeros(stat_shape + (2 * HEAD_DIM,), F32))
    carry, _ = lax.scan(page_step, init, page_table.T)
    causal = jnp.tril(jnp.ones((S, S), bool))
    m, l, acc = attend(carry, k.astype(F32), v.astype(F32), causal)
    o = jnp.transpose(acc / l[..., None], (0, 4, 1, 2, 3, 5))
    y = diff_finish(o, lam_vec, subln_g, w_out, h.dtype)
    rows = jnp.stack([k.reshape(bsz, S, DIFF_KV_HEADS, 2 * HEAD_DIM), v], axis=2)
    return y, rows


def swiglu(h, w_gu, w_down):
    gate, up = jnp.split(h @ w_gu, 2, axis=-1)
    return (jax.nn.silu(gate) * up) @ w_down


def moe(h, router, w_gu, w_down):
    shp = h.shape
    t = h.reshape(-1, shp[-1])
    logits = (t @ router).astype(F32)
    top_v, top_i = lax.top_k(logits, TOP_K)
    gates = jax.nn.softmax(top_v, axis=-1)
    comb = jnp.sum(jax.nn.one_hot(top_i, N_EXPERTS, dtype=F32) * gates[..., None], axis=1)
    y = jnp.zeros_like(t)
    for e in range(N_EXPERTS):
        y = y + (comb[:, e:e + 1] * swiglu(t, w_gu[e], w_down[e])).astype(t.dtype)
    return y.reshape(shp)


def setup_inputs(seed: int = 0) -> dict:
    key = jax.random.key(seed)
    keys = jax.random.split(key, 48)
    counter = [0]

    def nxt():
        kk = keys[counter[0]]
        counter[0] += 1
        return kk

    def nrm(shape, scale=1.0):
        return jax.random.normal(nxt(), shape, F32) * scale

    def gain(shape):
        return 1.0 + nrm(shape, 0.05)

    D = D_MODEL
    n_pages = PAST_LEN // PAGE_SIZE
    n_used = DEC_BATCH * n_pages
    n_phys = n_used + max(1, n_used // 4)
    n_dense = (DEPTH + 1) // 2
    n_moe = DEPTH // 2
    gla_in = 2 * GLA_HEADS * GLA_DK + 2 * GLA_HEADS * GLA_DV + GLA_RANK
    swa_in = (SWA_HEADS + 2 * SWA_KV_HEADS) * HEAD_DIM
    dil_in = len(DIL_GROUPS) * 3 * DIL_HEADS * HEAD_DIM
    diff_in = 2 * DIFF_HEADS * HEAD_DIM + 4 * DIFF_KV_HEADS * HEAD_DIM
    return {
        'x_prompt': nrm((BATCH, SEQ, D)),
        'x_sample': nrm((DEC_BATCH, DEC_SEQ, D)),
        'state_gla': nrm((DEC_BATCH, GLA_HEADS, GLA_DK, GLA_DV)),
        'cache_swa': nrm((DEC_BATCH, min(SWA_WINDOW, PAST_LEN), 2, SWA_KV_HEADS, HEAD_DIM)),
        'cache_dil1': nrm((DEC_BATCH, min(DIL_GROUPS[0][0], PAST_LEN), 2, DIL_HEADS, HEAD_DIM)),
        'cache_dil2': nrm((DEC_BATCH, min(DIL_GROUPS[1][0], PAST_LEN), 2, DIL_HEADS, HEAD_DIM)),
        'cache_dil3': nrm((DEC_BATCH, min(DIL_GROUPS[2][0], PAST_LEN), 2, DIL_HEADS, HEAD_DIM)),
        'cache_diff': nrm((n_phys, PAGE_SIZE, 2, DIFF_KV_HEADS, 2 * HEAD_DIM)),
        'page_table': jax.random.permutation(nxt(), n_phys)[:n_used].reshape(DEC_BATCH, n_pages).astype(jnp.int32),
        'c_prompt': nrm((BATCH, D)),
        'c_sample': nrm((DEC_BATCH, D)),
        'norm1_g': gain((DEPTH, D)),
        'norm2_g': gain((DEPTH, D)),
        'ada_w': nrm((DEPTH, D, 6 * D), 0.5 * D ** -0.5),
        'ada_b': nrm((DEPTH, 6 * D), 0.1),
        'gla_w_in': nrm((D, gla_in), D ** -0.5),
        'gla_w_gate2': nrm((GLA_RANK, GLA_HEADS * GLA_DK), GLA_RANK ** -0.5),
        'gla_b_gate': nrm((GLA_HEADS * GLA_DK,), 0.1),
        'gla_onorm_g': gain((GLA_DV,)),
        'gla_w_out': nrm((GLA_HEADS * GLA_DV, D), (GLA_HEADS * GLA_DV) ** -0.5),
        'swa_w_in': nrm((D, swa_in), D ** -0.5),
        'swa_q_norm': gain((HEAD_DIM,)),
        'swa_k_norm': gain((HEAD_DIM,)),
        'swa_sinks': nrm((SWA_HEADS,), 0.5),
        'swa_w_out': nrm((SWA_HEADS * HEAD_DIM, D), (SWA_HEADS * HEAD_DIM) ** -0.5),
        'dil_w_in': nrm((D, dil_in), D ** -0.5),
        'dil_q_norm': gain((HEAD_DIM,)),
        'dil_k_norm': gain((HEAD_DIM,)),
        'dil_w_out': nrm((DIL_HEADS * HEAD_DIM, D), (DIL_HEADS * HEAD_DIM) ** -0.5),
        'diff_w_in': nrm((D, diff_in), D ** -0.5),
        'diff_q_norm': gain((HEAD_DIM,)),
        'diff_k_norm': gain((HEAD_DIM,)),
        'diff_lambda': nrm((4, HEAD_DIM), 0.1),
        'diff_subln_g': gain((2 * HEAD_DIM,)),
        'diff_w_out': nrm((2 * DIFF_HEADS * HEAD_DIM, D), (2 * DIFF_HEADS * HEAD_DIM) ** -0.5),
        'ffn_w_gu': nrm((n_dense, D, 2 * D_FF), D ** -0.5),
        'ffn_w_down': nrm((n_dense, D_FF, D), D_FF ** -0.5),
        'moe_router': nrm((n_moe, D, N_EXPERTS), D ** -0.5),
        'moe_w_gu': nrm((n_moe, N_EXPERTS, D, 2 * D_EXPERT), D ** -0.5),
        'moe_w_down': nrm((n_moe, N_EXPERTS, D_EXPERT, D), D_EXPERT ** -0.5),
    }


def reference(x_prompt, x_sample, state_gla, cache_swa, cache_dil1, cache_dil2, cache_dil3, cache_diff,
              page_table, c_prompt, c_sample, norm1_g, norm2_g, ada_w, ada_b,
              gla_w_in, gla_w_gate2, gla_b_gate, gla_onorm_g, gla_w_out,
              swa_w_in, swa_q_norm, swa_k_norm, swa_sinks, swa_w_out,
              dil_w_in, dil_q_norm, dil_k_norm, dil_w_out,
              diff_w_in, diff_q_norm, diff_k_norm, diff_lambda, diff_subln_g, diff_w_out,
              ffn_w_gu, ffn_w_down, moe_router, moe_w_gu, moe_w_down):
    xp, xs = x_prompt, x_sample
    for i in range(DEPTH):
        mp = ada_mod(c_prompt, ada_w[i], ada_b[i])
        ms = ada_mod(c_sample, ada_w[i], ada_b[i])
        hp = adaln(xp, norm1_g[i], mp[0], mp[1])
        hs = adaln(xs, norm1_g[i], ms[0], ms[1])
        kind = i % N_MIXERS
        if kind == 0:
            s0 = jnp.zeros((xp.shape[0], GLA_HEADS, GLA_DK, GLA_DV), xp.dtype)
            op, gla_state_prompt = gla_mixer(hp, s0, gla_w_in, gla_w_gate2, gla_b_gate, gla_onorm_g, gla_w_out)
            osm, gla_state_sample = gla_mixer(hs, state_gla, gla_w_in, gla_w_gate2, gla_b_gate, gla_onorm_g, gla_w_out)
        elif kind == 1:
            op, swa_buf_prompt = swa_prompt(hp, swa_w_in, swa_q_norm, swa_k_norm, swa_sinks, swa_w_out)
            osm, swa_buf_sample = swa_sample(hs, cache_swa, swa_w_in, swa_q_norm, swa_k_norm, swa_sinks, swa_w_out)
        elif kind == 2:
            op, (dil1_buf_prompt, dil2_buf_prompt, dil3_buf_prompt) = dil_prompt(
                hp, dil_w_in, dil_q_norm, dil_k_norm, dil_w_out)
            osm, (dil1_buf_sample, dil2_buf_sample, dil3_buf_sample) = dil_sample(
                hs, (cache_dil1, cache_dil2, cache_dil3), dil_w_in, dil_q_norm, dil_k_norm, dil_w_out)
        else:
            op, diff_rows_prompt = diff_prompt(hp, diff_w_in, diff_q_norm, diff_k_norm, diff_lambda,
                                               diff_subln_g, diff_w_out)
            osm, diff_rows_sample = diff_sample(hs, cache_diff, page_table, diff_w_in, diff_q_norm, diff_k_norm,
                                                diff_lambda, diff_subln_g, diff_w_out)
        xp = xp + mp[2] * op
        xs = xs + ms[2] * osm
        hp = adaln(xp, norm2_g[i], mp[3], mp[4])
        hs = adaln(xs, norm2_g[i], ms[3], ms[4])
        j = i // 2
        if i % 2 == 0:
            fp = swiglu(hp, ffn_w_gu[j], ffn_w_down[j])
            fs = swiglu(hs, ffn_w_gu[j], ffn_w_down[j])
        else:
            fp = moe(hp, moe_router[j], moe_w_gu[j], moe_w_down[j])
            fs = moe(hs, moe_router[j], moe_w_gu[j], moe_w_down[j])
        xp = xp + mp[5] * fp
        xs = xs + ms[5] * fs
    return (xp, xs, gla_state_prompt, gla_state_sample, swa_buf_prompt, swa_buf_sample,
            dil1_buf_prompt, dil1_buf_sample, dil2_buf_prompt, dil2_buf_sample,
            dil3_buf_prompt, dil3_buf_sample, diff_rows_prompt, diff_rows_sample)
```

```python
import functools
import math

import jax
import jax.numpy as jnp
from jax import lax
from jax.experimental import pallas as pl
from jax.experimental.pallas import tpu as pltpu

F32 = jnp.float32
BF16 = jnp.bfloat16

EPS = 1e-6
NEG = -1e30
BLK = 128
HEAD_DIM = 64

GLA_HEADS = 4
GLA_RANK = 16
GLA_NORMALIZER = 16.0
GLA_CHUNK = 64

SWA_KV_HEADS = 4
SWA_WINDOW = 128
DIL_GROUPS = ((128, 1), (512, 4), (2048, 16))
DIL_HEADS = 8
DIFF_KV_HEADS = 4
DIFF_GROUP = 2
LAMBDA_INIT = 0.8 - 0.6 * math.exp(-0.3 * 3)
N_EXPERTS = 8
PAGE_SIZE = 128

LANES = 128
VMEM_LIMIT = 56 * 1024 * 1024


def _cp(*sem):
    return pltpu.CompilerParams(dimension_semantics=sem, vmem_limit_bytes=VMEM_LIMIT)


def _dot(a, b):
    return jnp.dot(a, b, preferred_element_type=F32)


def _dot_nt(a, b):
    return lax.dot_general(a, b, (((1,), (1,)), ((), ())), preferred_element_type=F32)


def _dot_tn(a, b):
    return lax.dot_general(a, b, (((0,), (0,)), ((), ())), preferred_element_type=F32)


def _silu(x):
    return x / (1.0 + jnp.exp(-x))


def _adaln(x, g, shift, scale):
    y = x * lax.rsqrt(jnp.mean(x * x, axis=-1, keepdims=True) + EPS)
    return y * g * (1.0 + scale) + shift


class _Group:
    def __init__(self, n_batch, rows_per_batch, mods, tm_max=1024):
        self.m = n_batch * rows_per_batch
        self.prompt = rows_per_batch >= 256
        d6 = mods.shape[-1]
        if self.prompt:
            self.tm = math.gcd(rows_per_batch, tm_max)
            self.tiles_per_batch = rows_per_batch // self.tm
            self.mods = mods.reshape(n_batch, 1, d6)
        else:
            self.tm = math.gcd(self.m, tm_max)
            self.mods = jnp.repeat(mods, rows_per_batch, axis=0)
        self.d = d6 // 6

    def mod_spec(self, k):
        d = self.d
        if self.prompt:
            tpb = self.tiles_per_batch
            return pl.BlockSpec((None, 1, d), lambda i, *_: (i // tpb, 0, k))
        return pl.BlockSpec((self.tm, d), lambda i, *_: (i, k))


def _ada_kernel(c_ref, w_ref, b_ref, o_ref):
    a = _silu(c_ref[...]).astype(BF16)
    o_ref[...] = _dot(a, w_ref[...].astype(BF16)) + b_ref[...]


def _ada(c_all, ada_w, ada_b):
    depth, d, d6 = ada_w.shape
    nb = c_all.shape[0]
    tn = 1024
    return pl.pallas_call(
        _ada_kernel,
        grid=(depth, d6 // tn),
        in_specs=[
            pl.BlockSpec((nb, d), lambda l, j: (0, 0)),
            pl.BlockSpec((None, d, tn), lambda l, j: (l, 0, j)),
            pl.BlockSpec((None, 1, tn), lambda l, j: (l, 0, j)),
        ],
        out_specs=pl.BlockSpec((None, nb, tn), lambda l, j: (l, 0, j)),
        out_shape=jax.ShapeDtypeStruct((depth, nb, d6), F32),
        compiler_params=_cp("parallel", "parallel"),
        name="ada_mod",
    )(c_all, ada_w, ada_b.reshape(depth, 1, d6))


def _inproj_kernel(x_ref, g_ref, sh_ref, sc_ref, w_ref, *rest, norm):
    if norm:
        flag_ref, gain_ref, o_ref, h_ref = rest
    else:
        o_ref, h_ref = rest
    j = pl.program_id(1)

    @pl.when(j == 0)
    def _():
        h_ref[...] = _adaln(x_ref[...], g_ref[...], sh_ref[...], sc_ref[...]).astype(BF16)

    acc = _dot(h_ref[...], w_ref[...].astype(BF16))
    if not norm:
        o_ref[...] = acc
        return
    tn = acc.shape[1]
    lo = lax.broadcasted_iota(jnp.int32, (1, LANES), 1) < HEAD_DIM
    for c in range(tn // LANES):
        sl = slice(c * LANES, (c + 1) * LANES)
        a = acc[:, sl]
        sq = a * a
        s_lo = jnp.sum(jnp.where(lo, sq, 0.0), axis=-1, keepdims=True)
        s_hi = jnp.sum(jnp.where(lo, 0.0, sq), axis=-1, keepdims=True)
        r = lax.rsqrt(jnp.where(lo, s_lo, s_hi) * (1.0 / HEAD_DIM) + EPS)
        o_ref[:, sl] = a * jnp.where(flag_ref[:, sl] > 0.0, r, 1.0) * gain_ref[:, sl]


def _inproj(grp, x, norm_g, k_shift, w, tn, norm=None):
    m, d = x.shape
    n = w.shape[1]
    tm = grp.tm
    in_specs = [
        pl.BlockSpec((tm, d), lambda i, j: (i, 0)),
        pl.BlockSpec((1, d), lambda i, j: (0, 0)),
        grp.mod_spec(k_shift),
        grp.mod_spec(k_shift + 1),
        pl.BlockSpec((d, tn), lambda i, j: (0, j)),
    ]
    args = [x, norm_g.reshape(1, d), grp.mods, grp.mods, w]
    if norm is not None:
        in_specs += [pl.BlockSpec((1, tn), lambda i, j: (0, j))] * 2
        args += list(norm)
    return pl.pallas_call(
        functools.partial(_inproj_kernel, norm=norm is not None),
        grid=(m // tm, n // tn),
        in_specs=in_specs,
        out_specs=pl.BlockSpec((tm, tn), lambda i, j: (i, j)),
        out_shape=jax.ShapeDtypeStruct((m, n), F32),
        scratch_shapes=[pltpu.VMEM((tm, d), BF16)],
        compiler_params=_cp("parallel", "arbitrary"),
        name="adaln_inproj",
    )(*args)


def _norm_rows(pieces):
    flags, gains = [], []
    for width, g, scale in pieces:
        if g is None:
            flags.append(jnp.zeros((width,), F32))
            gains.append(jnp.ones((width,), F32))
        else:
            flags.append(jnp.ones((width,), F32))
            gains.append(jnp.tile(g.astype(F32) * scale, width // HEAD_DIM))
    return jnp.concatenate(flags)[None, :], jnp.concatenate(gains)[None, :]


def _outproj_kernel(x_ref, o_ref, gt_ref, w_ref, out_ref):
    y = _dot(o_ref[...].astype(BF16), w_ref[...].astype(BF16))
    out_ref[...] = x_ref[...] + gt_ref[...] * y


def _outproj(grp, x, o, w, k_gate):
    m, d = x.shape
    kdim = w.shape[0]
    tm = grp.tm
    return pl.pallas_call(
        _outproj_kernel,
        grid=(m // tm,),
        in_specs=[
            pl.BlockSpec((tm, d), lambda i: (i, 0)),
            pl.BlockSpec((tm, kdim), lambda i: (i, 0)),
            grp.mod_spec(k_gate),
            pl.BlockSpec((kdim, d), lambda i: (0, 0)),
        ],
        out_specs=pl.BlockSpec((tm, d), lambda i: (i, 0)),
        out_shape=jax.ShapeDtypeStruct((m, d), F32),
        compiler_params=_cp("parallel"),
        name="outproj",
    )(x, o, grp.mods, w)


def _outproj_mix_kernel(x_ref, o0_ref, o1_ref, o2_ref, l0_ref, l1_ref, l2_ref, gt_ref, w_ref, out_ref):
    l0, l1, l2 = l0_ref[...], l1_ref[...], l2_ref[...]
    mx = jnp.maximum(jnp.maximum(l0, l1), l2)
    e0, e1, e2 = jnp.exp(l0 - mx), jnp.exp(l1 - mx), jnp.exp(l2 - mx)
    o = (e0 * o0_ref[...] + e1 * o1_ref[...] + e2 * o2_ref[...]) / (e0 + e1 + e2)
    y = _dot(o.astype(BF16), w_ref[...].astype(BF16))
    out_ref[...] = x_ref[...] + gt_ref[...] * y


def _outproj_mix(grp, x, outs, lses, w, k_gate):
    m, d = x.shape
    kdim = w.shape[0]
    tm = grp.tm
    row = pl.BlockSpec((tm, kdim), lambda i: (i, 0))
    return pl.pallas_call(
        _outproj_mix_kernel,
        grid=(m // tm,),
        in_specs=[pl.BlockSpec((tm, d), lambda i: (i, 0))] + [row] * 6
        + [grp.mod_spec(k_gate), pl.BlockSpec((kdim, d), lambda i: (0, 0))],
        out_specs=pl.BlockSpec((tm, d), lambda i: (i, 0)),
        out_shape=jax.ShapeDtypeStruct((m, d), F32),
        compiler_params=_cp("parallel"),
        name="outproj_mix",
    )(x, *outs, *lses, grp.mods, w)


def _ffn_kernel(x_ref, g_ref, sh_ref, sc_ref, gt_ref, wg_ref, wu_ref, wd_ref, o_ref, h_ref):
    f = pl.program_id(1)
    nf = pl.num_programs(1)

    @pl.when(f == 0)
    def _():
        h_ref[...] = _adaln(x_ref[...], g_ref[...], sh_ref[...], sc_ref[...]).astype(BF16)
        o_ref[...] = jnp.zeros_like(o_ref)

    h = h_ref[...]
    gate = _dot(h, wg_ref[...].astype(BF16))
    up = _dot(h, wu_ref[...].astype(BF16))
    a = (_silu(gate) * up).astype(BF16)
    o_ref[...] += _dot(a, wd_ref[...].astype(BF16))

    @pl.when(f == nf - 1)
    def _():
        o_ref[...] = x_ref[...] + gt_ref[...] * o_ref[...]


def _ffn(grp, x, norm_g, w_gu, w_down, tf):
    m, d = x.shape
    dff = w_down.shape[0]
    tm = grp.tm
    nf = dff // tf
    return pl.pallas_call(
        _ffn_kernel,
        grid=(m // tm, nf),
        in_specs=[
            pl.BlockSpec((tm, d), lambda i, f: (i, 0)),
            pl.BlockSpec((1, d), lambda i, f: (0, 0)),
            grp.mod_spec(3), grp.mod_spec(4), grp.mod_spec(5),
            pl.BlockSpec((d, tf), lambda i, f: (0, f)),
            pl.BlockSpec((d, tf), lambda i, f: (0, f + nf)),
            pl.BlockSpec((tf, d), lambda i, f: (f, 0)),
        ],
        out_specs=pl.BlockSpec((tm, d), lambda i, f: (i, 0)),
        out_shape=jax.ShapeDtypeStruct((m, d), F32),
        scratch_shapes=[pltpu.VMEM((tm, d), BF16)],
        compiler_params=_cp("parallel", "arbitrary"),
        name="ffn_swiglu",
    )(x, norm_g.reshape(1, d), grp.mods, grp.mods, grp.mods, w_gu, w_gu, w_down)


def _route_top2(logits):
    lane = lax.broadcasted_iota(jnp.int32, logits.shape, 1)
    lg = jnp.where(lane < N_EXPERTS, logits, -jnp.inf)
    m1 = jnp.max(lg, axis=-1, keepdims=True)
    i1 = jnp.min(jnp.where(lg == m1, lane, LANES), axis=-1, keepdims=True)
    lg2 = jnp.where(lane == i1, -jnp.inf, lg)
    m2 = jnp.max(lg2, axis=-1, keepdims=True)
    i2 = jnp.min(jnp.where(lg2 == m2, lane, LANES), axis=-1, keepdims=True)
    e2 = jnp.exp(m2 - m1)
    g1 = 1.0 / (1.0 + e2)
    g2 = e2 / (1.0 + e2)
    return jnp.where(lane == i1, g1, 0.0) + jnp.where(lane == i2, g2, 0.0)


def _moe_kernel(x_ref, g_ref, sh_ref, sc_ref, gt_ref, r_ref, wg_ref, wu_ref, wd_ref, o_ref, h_ref, comb_ref):
    e = pl.program_id(1)
    f = pl.program_id(2)
    ne = pl.num_programs(1)
    nf = pl.num_programs(2)

    @pl.when((e == 0) & (f == 0))
    def _():
        h = _adaln(x_ref[...], g_ref[...], sh_ref[...], sc_ref[...]).astype(BF16)
        h_ref[...] = h
        comb_ref[...] = _route_top2(_dot(h, r_ref[...].astype(BF16)))
        o_ref[...] = jnp.zeros_like(o_ref)

    lane = lax.broadcasted_iota(jnp.int32, comb_ref.shape, 1)
    ce = jnp.sum(jnp.where(lane == e, comb_ref[...], 0.0), axis=-1, keepdims=True)
    h = h_ref[...]
    gate = _dot(h, wg_ref[...].astype(BF16))
    up = _dot(h, wu_ref[...].astype(BF16))
    a = (_silu(gate) * up * ce).astype(BF16)
    o_ref[...] += _dot(a, wd_ref[...].astype(BF16))

    @pl.when((e == ne - 1) & (f == nf - 1))
    def _():
        o_ref[...] = x_ref[...] + gt_ref[...] * o_ref[...]


def _moe(grp, x, norm_g, router, w_gu, w_down, tf):
    m, d = x.shape
    ne, dff, _ = w_down.shape
    tm = grp.tm
    nf = dff // tf
    router_p = jnp.pad(router, ((0, 0), (0, LANES - ne)))
    return pl.pallas_call(
        _moe_kernel,
        grid=(m // tm, ne, nf),
        in_specs=[
            pl.BlockSpec((tm, d), lambda i, e, f: (i, 0)),
            pl.BlockSpec((1, d), lambda i, e, f: (0, 0)),
            grp.mod_spec(3), grp.mod_spec(4), grp.mod_spec(5),
            pl.BlockSpec((d, LANES), lambda i, e, f: (0, 0)),
            pl.BlockSpec((None, d, tf), lambda i, e, f: (e, 0, f)),
            pl.BlockSpec((None, d, tf), lambda i, e, f: (e, 0, f + nf)),
            pl.BlockSpec((None, tf, d), lambda i, e, f: (e, f, 0)),
        ],
        out_specs=pl.BlockSpec((tm, d), lambda i, e, f: (i, 0)),
        out_shape=jax.ShapeDtypeStruct((m, d), F32),
        scratch_shapes=[pltpu.VMEM((tm, d), BF16), pltpu.VMEM((tm, LANES), F32)],
        compiler_params=_cp("parallel", "arbitrary", "arbitrary"),
        name="moe",
    )(x, norm_g.reshape(1, d), grp.mods, grp.mods, grp.mods, router_p, w_gu, w_gu, w_down)


def _gla_kernel(p_ref, wg2_ref, bg_ref, on_ref, *rest, chunk, n_chunk, n_valid, has_s0):
    rest = list(rest)
    s0_ref = rest.pop(0) if has_s0 else None
    o_ref, sfin_ref, s_ref = rest[:3]
    if n_valid < chunk:
        src = rest[3]
        src[...] = jnp.zeros_like(src)
        src[:n_valid, :] = p_ref[...]
    else:
        src = p_ref
    li = pl.program_id(1)
    nl = pl.num_programs(1)
    nq = GLA_HEADS * 128
    nv = GLA_HEADS * 256
    dk, dv = 128, 256

    @pl.when(li == 0)
    def _():
        if has_s0:
            s_ref[...] = s0_ref[...]
        else:
            s_ref[...] = jnp.zeros_like(s_ref)

    c = chunk
    row = lax.broadcasted_iota(jnp.int32, (c, c), 0)
    col = lax.broadcasted_iota(jnp.int32, (c, c), 1)
    causal = row >= col
    tri = jnp.where(causal, 1.0, 0.0).astype(BF16)
    ones = jnp.ones((c, dv), BF16)
    live = None
    if n_valid < c:
        live = lax.broadcasted_iota(jnp.int32, (c, 1), 0) < n_valid
    for ci in range(n_chunk):
        rows = slice(ci * c, (ci + 1) * c)
        a_lr = src[rows, 2 * nq + 2 * nv:]
        z = _dot(a_lr.astype(BF16), wg2_ref[...].astype(BF16)) + bg_ref[...]
        la = (jnp.minimum(z, 0.0) - jnp.log1p(jnp.exp(-jnp.abs(z)))) * (1.0 / GLA_NORMALIZER)
        if live is not None:
            la = jnp.where(live, la, 0.0)
        hi = la.astype(BF16)
        lo = (la - hi.astype(F32)).astype(BF16)
        b = _dot(tri, hi) + _dot(tri, lo)
        for h in range(GLA_HEADS):
            ks = slice(h * dk, (h + 1) * dk)
            vs = slice(h * dv, (h + 1) * dv)
            bh = b[:, ks]
            bl = bh[c - 1:c, :]
            qh = src[rows, h * dk:(h + 1) * dk] * (dk ** -0.5)
            kh = src[rows, nq + h * dk:nq + (h + 1) * dk]
            vh = src[rows, 2 * nq + h * dv:2 * nq + (h + 1) * dv]
            gh = src[rows, 2 * nq + nv + h * dv:2 * nq + nv + (h + 1) * dv]
            vb = vh.astype(BF16)
            qd = (qh * jnp.exp(bh)).astype(BF16)
            ki = (kh * jnp.exp(-bh)).astype(BF16)
            kd = (kh * jnp.exp(bl - bh)).astype(BF16)
            att = jnp.where(causal, _dot_nt(qd, ki), 0.0).astype(BF16)
            s = s_ref[h]
            o = _dot(att, vb) + _dot(qd, s.astype(BF16))
            dl = _dot_tn(hi[:, ks], ones) + _dot_tn(lo[:, ks], ones)
            s_ref[h] = jnp.exp(dl) * s + _dot_tn(kd, vb)
            on = o * lax.rsqrt(jnp.mean(o * o, axis=-1, keepdims=True) + EPS) * on_ref[...]
            res = on * _silu(gh)
            if n_valid < c:
                o_ref[:, vs] = res[:n_valid]
            else:
                o_ref[rows, vs] = res

    @pl.when(li == nl - 1)
    def _():
        sfin_ref[...] = s_ref[...]


def _gla(p, n_batch, seq, w_gate2_p, b_gate, onorm_g, s0):
    n_in = p.shape[1]
    nv = GLA_HEADS * 256
    c_ref = math.gcd(seq, GLA_CHUNK)
    if seq >= GLA_CHUNK:
        chunk, rows, n_valid = c_ref, math.gcd(seq, 256), c_ref
        scratch = []
    else:
        chunk, rows, n_valid = GLA_CHUNK, seq, seq
        scratch = [pltpu.VMEM((chunk, n_in), F32)]
    n_l = seq // rows
    p3 = p.reshape(n_batch, seq, n_in)
    has_s0 = s0 is not None
    st_spec = pl.BlockSpec((None, GLA_HEADS, 128, 256), lambda b, l: (b, 0, 0, 0))
    in_specs = [
        pl.BlockSpec((None, rows, n_in), lambda b, l: (b, l, 0)),
        pl.BlockSpec((LANES, GLA_HEADS * 128), lambda b, l: (0, 0)),
        pl.BlockSpec((1, GLA_HEADS * 128), lambda b, l: (0, 0)),
        pl.BlockSpec((1, 256), lambda b, l: (0, 0)),
    ]
    args = [p3, w_gate2_p, b_gate.reshape(1, -1), onorm_g.reshape(1, -1)]
    if has_s0:
        in_specs.append(st_spec)
        args.append(s0)
    o, s_fin = pl.pallas_call(
        functools.partial(_gla_kernel, chunk=chunk, n_chunk=max(rows // chunk, 1), n_valid=n_valid,
                          has_s0=has_s0),
        grid=(n_batch, n_l),
        in_specs=in_specs,
        out_specs=[pl.BlockSpec((None, rows, nv), lambda b, l: (b, l, 0)), st_spec],
        out_shape=[jax.ShapeDtypeStruct((n_batch, seq, nv), F32),
                   jax.ShapeDtypeStruct((n_batch, GLA_HEADS, 128, 256), F32)],
        scratch_shapes=[pltpu.VMEM((GLA_HEADS, 128, 256), F32)] + scratch,
        compiler_params=_cp("parallel", "arbitrary"),
        name="gla",
    )(*args)
    return o.reshape(n_batch * seq, nv), s_fin


def _banded_kernel(q_ref, kc_ref, kp_ref, vc_ref, vp_ref, *rest, hk, grp, window, has_sink, want_lse):
    rest = list(rest)
    sink_ref = rest.pop(0) if has_sink else None
    o_ref = rest.pop(0)
    lse_ref = rest.pop(0) if want_lse else None
    n = pl.program_id(2)
    i = lax.broadcasted_iota(jnp.int32, (BLK, BLK), 0)
    j = lax.broadcasted_iota(jnp.int32, (BLK, BLK), 1)
    valid_c = (j <= i) & (i - j <= window)
    valid_p = i - j + BLK + jnp.where(n > 0, 0, 2 * BLK) <= window
    dh = HEAD_DIM
    for h in range(hk):
        hs = slice(h * dh, (h + 1) * dh)
        kc = kc_ref[:, hs].astype(BF16)
        kp = kp_ref[:, hs].astype(BF16)
        vc = vc_ref[:, hs].astype(BF16)
        vp = vp_ref[:, hs].astype(BF16)
        for g in range(grp):
            qi = h * grp + g
            qs = slice(qi * dh, (qi + 1) * dh)
            q = q_ref[:, qs].astype(BF16)
            sc = jnp.where(valid_c, _dot_nt(q, kc), NEG)
            sp = jnp.where(valid_p, _dot_nt(q, kp), NEG)
            m = jnp.maximum(jnp.max(sc, axis=-1, keepdims=True), jnp.max(sp, axis=-1, keepdims=True))
            if has_sink:
                sk = sink_ref[:, qi:qi + 1]
                m = jnp.maximum(m, sk)
            pc = jnp.exp(sc - m)
            pp = jnp.exp(sp - m)
            l = jnp.sum(pc, axis=-1, keepdims=True) + jnp.sum(pp, axis=-1, keepdims=True)
            if has_sink:
                l = l + jnp.exp(sk - m)
            o = (_dot(pc.astype(BF16), vc) + _dot(pp.astype(BF16), vp)) / l
            o_ref[:, qs] = o
            if want_lse:
                lse_ref[:, qs] = jnp.broadcast_to(m + jnp.log(l), (BLK, dh))


def _banded(p, n_batch, seq, dil, q_blk, k_blk, v_blk, hk, grp, window, sinks=None, want_lse=False):
    n_in = p.shape[1]
    lr = seq // dil
    assert lr % BLK == 0
    nb = lr // BLK
    p4 = p.reshape(n_batch, lr, dil * n_in)
    qw, qi = q_blk
    kw, ki = k_blk
    vw, vi = v_blk
    assert dil == 1 or (n_in % qw == 0 and n_in % kw == 0 and n_in % vw == 0)

    def cur(w, ci):
        return pl.BlockSpec((None, BLK, w), lambda b, r, n: (b, n, r * (n_in // w) + ci))

    def prev(w, ci):
        return pl.BlockSpec((None, BLK, w), lambda b, r, n: (b, jnp.maximum(n - 1, 0), r * (n_in // w) + ci))

    in_specs = [cur(qw, qi), cur(kw, ki), prev(kw, ki), cur(vw, vi), prev(vw, vi)]
    args = [p4] * 5
    if sinks is not None:
        in_specs.append(pl.BlockSpec((1, sinks.shape[0]), lambda b, r, n: (0, 0)))
        args.append(sinks.reshape(1, -1).astype(F32))
    o_spec = pl.BlockSpec((None, BLK, qw), lambda b, r, n: (b, n, r))
    o_shape = jax.ShapeDtypeStruct((n_batch, lr, dil * qw), F32)
    outs = pl.pallas_call(
        functools.partial(_banded_kernel, hk=hk, grp=grp, window=window,
                          has_sink=sinks is not None, want_lse=want_lse),
        grid=(n_batch, dil, nb),
        in_specs=in_specs,
        out_specs=[o_spec, o_spec] if want_lse else [o_spec],
        out_shape=[o_shape, o_shape] if want_lse else [o_shape],
        compiler_params=_cp("parallel", "parallel", "arbitrary"),
        name="banded_attn",
    )(*args)
    return [t.reshape(n_batch * seq, qw) for t in outs]


def _swa_sample_kernel(q_ref, kn_ref, vn_ref, buf_ref, sink_ref, o_ref, nbuf_ref, kv_ref, *, bb, hk, grp):
    s_len = q_ref.shape[1]
    lb = buf_ref.shape[1]
    dh = HEAD_DIM
    kvw = hk * dh
    n_key = kv_ref.shape[0]
    s_i = lax.broadcasted_iota(jnp.int32, (s_len, n_key), 0)
    j_i = lax.broadcasted_iota(jnp.int32, (s_len, n_key), 1)
    dist = lb + s_i - j_i
    valid = (dist >= 0) & (dist <= SWA_WINDOW)
    kv_ref[lb + s_len:, :] = jnp.zeros((n_key - lb - s_len, 2 * kvw), F32)
    for b in range(bb):
        kv_ref[:lb, :] = buf_ref[b]
        kv_ref[lb:lb + s_len, :kvw] = kn_ref[b]
        kv_ref[lb:lb + s_len, kvw:] = vn_ref[b]
        for h in range(hk):
            k = kv_ref[:, h * dh:(h + 1) * dh]
            v = kv_ref[:, kvw + h * dh:kvw + (h + 1) * dh]
            for g in range(grp):
                qi = h * grp + g
                qs = slice(qi * dh, (qi + 1) * dh)
                s = jnp.where(valid, _dot_nt(q_ref[b, :, qs], k), NEG)
                sk = sink_ref[:, qi:qi + 1]
                m = jnp.maximum(jnp.max(s, axis=-1, keepdims=True), sk)
                p = jnp.exp(s - m)
                l = jnp.sum(p, axis=-1, keepdims=True) + jnp.exp(sk - m)
                o_ref[b, :, qs] = _dot(p, v) / l
        nbuf_ref[b] = kv_ref[s_len:lb + s_len, :]


def _swa_sample(p, n_batch, s_len, buf, sinks, hk, grp):
    n_in = p.shape[1]
    lb = buf.shape[1]
    dh = HEAD_DIM
    qw, kvw = hk * grp * dh, hk * dh
    bb = math.gcd(n_batch, 4)
    p3 = p.reshape(n_batch, s_len, n_in)
    buf3 = buf.reshape(n_batch, lb, 2 * kvw)
    o, nbuf = pl.pallas_call(
        functools.partial(_swa_sample_kernel, bb=bb, hk=hk, grp=grp),
        grid=(n_batch // bb,),
        in_specs=[
            pl.BlockSpec((bb, s_len, qw), lambda i: (i, 0, 0)),
            pl.BlockSpec((bb, s_len, kvw), lambda i: (i, 0, qw // kvw)),
            pl.BlockSpec((bb, s_len, kvw), lambda i: (i, 0, qw // kvw + 1)),
            pl.BlockSpec((bb, lb, 2 * kvw), lambda i: (i, 0, 0)),
            pl.BlockSpec((1, hk * grp), lambda i: (0, 0)),
        ],
        out_specs=[pl.BlockSpec((bb, s_len, qw), lambda i: (i, 0, 0)),
                   pl.BlockSpec((bb, lb, 2 * kvw), lambda i: (i, 0, 0))],
        out_shape=[jax.ShapeDtypeStruct((n_batch, s_len, qw), F32),
                   jax.ShapeDtypeStruct((n_batch, lb, 2 * kvw), F32)],
        scratch_shapes=[pltpu.VMEM((2 * lb, 2 * kvw), F32)],
        compiler_params=_cp("parallel"),
        name="swa_decode",
    )(p3, p3, p3, buf3, sinks.reshape(1, -1).astype(F32))
    return o.reshape(n_batch * s_len, qw), nbuf.reshape(buf.shape)


def _dil_sample_kernel(q_ref, kn_ref, vn_ref, buf_ref, o_ref, lse_ref, nbuf_ref, *, bb, dil, n_heads):
    s_len = q_ref.shape[1]
    n_t = buf_ref.shape[1]
    dh = HEAD_DIM
    w = n_heads * dh
    seg = lax.broadcasted_iota(jnp.int32, (n_heads, w), 1) // dh == lax.broadcasted_iota(jnp.int32, (n_heads, w), 0)
    t_idx = lax.broadcasted_iota(jnp.int32, (n_heads, n_t), 1)
    for b in range(bb):
        o_rows, l_rows = [], []
        for s in range(s_len):
            r, t0 = s % dil, s // dil
            qbd = jnp.where(seg, q_ref[b, s:s + 1, :], 0.0)
            kc = buf_ref[b, :, r * 2 * w:r * 2 * w + w]
            vc = buf_ref[b, :, r * 2 * w + w:(r + 1) * 2 * w]
            sc = jnp.where(t_idx >= t0, _dot_nt(qbd.astype(BF16), kc.astype(BF16)), NEG)
            m = jnp.max(sc, axis=-1, keepdims=True)
            news = [s2 for s2 in range(s + 1) if (s - s2) % dil == 0 and (s - s2) // dil <= n_t]
            sn = [jnp.sum(qbd * kn_ref[b, s2:s2 + 1, :], axis=-1, keepdims=True) for s2 in news]
            for t in sn:
                m = jnp.maximum(m, t)
            pc = jnp.exp(sc - m)
            l = jnp.sum(pc, axis=-1, keepdims=True)
            pv = _dot(pc.astype(BF16), vc.astype(BF16))
            for s2, t in zip(news, sn):
                pn = jnp.exp(t - m)
                l = l + pn
                pv = pv + pn * vn_ref[b, s2:s2 + 1, :]
            o_rows.append(jnp.sum(jnp.where(seg, pv / l, 0.0), axis=0, keepdims=True))
            l_rows.append(jnp.sum(jnp.where(seg, m + jnp.log(l), 0.0), axis=0, keepdims=True))
        o_ref[b] = jnp.concatenate(o_rows, axis=0)
        lse_ref[b] = jnp.concatenate(l_rows, axis=0)
        sh_t, sh_r = s_len // dil, s_len % dil
        cut = (dil - sh_r) * 2 * w
        nbuf_ref[b, :n_t - sh_t, :cut] = buf_ref[b, sh_t:, sh_r * 2 * w:]
        if sh_r:
            nbuf_ref[b, :n_t - sh_t - 1, cut:] = buf_ref[b, sh_t + 1:, :sh_r * 2 * w]
        for s in range(s_len):
            t, r = divmod(n_t * dil - s_len + s, dil)
            nbuf_ref[b, t:t + 1, r * 2 * w:r * 2 * w + w] = kn_ref[b, s:s + 1, :]
            nbuf_ref[b, t:t + 1, r * 2 * w + w:(r + 1) * 2 * w] = vn_ref[b, s:s + 1, :]


def _dil_sample(p, n_batch, s_len, buf, gi, win, dil):
    n_in = p.shape[1]
    lb = buf.shape[1]
    assert lb == win and lb % dil == 0 and s_len <= dil * (lb // dil)
    w = DIL_HEADS * HEAD_DIM
    bb = max(1, min(math.gcd(n_batch, 8), (2 * 1024 * 1024) // (lb * 2 * w * 4)))
    p3 = p.reshape(n_batch, s_len, n_in)
    n_t = lb // dil
    buf3 = buf.reshape(n_batch, n_t, dil * 2 * w)
    buf_spec = pl.BlockSpec((bb, n_t, dil * 2 * w), lambda i: (i, 0, 0))
    row = pl.BlockSpec((bb, s_len, w), lambda i: (i, 0, 0))
    o, lse, nbuf = pl.pallas_call(
        functools.partial(_dil_sample_kernel, bb=bb, dil=dil, n_heads=DIL_HEADS),
        grid=(n_batch // bb,),
        in_specs=[
            pl.BlockSpec((bb, s_len, w), lambda i: (i, 0, 3 * gi)),
            pl.BlockSpec((bb, s_len, w), lambda i: (i, 0, 3 * gi + 1)),
            pl.BlockSpec((bb, s_len, w), lambda i: (i, 0, 3 * gi + 2)),
            buf_spec,
        ],
        out_specs=[row, row, buf_spec],
        out_shape=[jax.ShapeDtypeStruct((n_batch, s_len, w), F32),
                   jax.ShapeDtypeStruct((n_batch, s_len, w), F32),
                   jax.ShapeDtypeStruct(buf3.shape, F32)],
        compiler_params=_cp("parallel"),
        name="dil_decode",
    )(p3, p3, p3, buf3)
    return o.reshape(n_batch * s_len, w), lse.reshape(n_batch * s_len, w), nbuf.reshape(buf.shape)


def _diff_lambda(lam_ref):
    lam = lam_ref[...]
    a = jnp.sum(lam[0:1] * lam[1:2], axis=-1, keepdims=True)
    b = jnp.sum(lam[2:3] * lam[3:4], axis=-1, keepdims=True)
    return jnp.exp(a) - jnp.exp(b) + LAMBDA_INIT


def _diff_head_out(a1, a2, lam, sub_g):
    o = a1 - lam * a2
    return o * lax.rsqrt(jnp.mean(o * o, axis=-1, keepdims=True) + EPS) * sub_g * (1.0 - LAMBDA_INIT)


def _diff_prompt_kernel(q_ref, kv_ref, lam_ref, sub_ref, o_ref, m_ref, l_ref, acc_ref, *, tq):
    qi = pl.program_id(1)
    dh = HEAD_DIM
    dv = 2 * dh
    kw = DIFF_KV_HEADS * dv
    rows2 = DIFF_GROUP * tq
    r_i = lax.broadcasted_iota(jnp.int32, (rows2, tq), 0) % tq
    c_i = lax.broadcasted_iota(jnp.int32, (rows2, tq), 1)
    diag = c_i <= r_i
    lam = _diff_lambda(lam_ref)
    for h in range(DIFF_KV_HEADS):
        for mp in range(2):
            q = jnp.concatenate(
                [q_ref[:, ((h * DIFF_GROUP + g) * 2 + mp) * dh:((h * DIFF_GROUP + g) * 2 + mp + 1) * dh]
                 for g in range(DIFF_GROUP)], axis=0).astype(BF16)
            m_ref[...] = jnp.full_like(m_ref, NEG)
            l_ref[...] = jnp.zeros_like(l_ref)
            acc_ref[...] = jnp.zeros_like(acc_ref)

            def step(kb, carry, masked):
                r0 = pl.multiple_of(kb * tq, tq)
                k = kv_ref[pl.ds(r0, tq), h * dv + mp * dh:h * dv + (mp + 1) * dh].astype(BF16)
                v = kv_ref[pl.ds(r0, tq), kw + h * dv:kw + (h + 1) * dv].astype(BF16)
                s = _dot_nt(q, k)
                if masked:
                    s = jnp.where(diag, s, NEG)
                m_old = m_ref[...]
                m_new = jnp.maximum(m_old, jnp.max(s, axis=-1, keepdims=True))
                corr = jnp.exp(m_old - m_new)
                p = jnp.exp(s - m_new)
                l_ref[...] = l_ref[...] * corr + jnp.sum(p, axis=-1, keepdims=True)
                acc_ref[...] = acc_ref[...] * corr + _dot(p.astype(BF16), v)
                m_ref[...] = m_new
                return carry

            lax.fori_loop(0, qi, functools.partial(step, masked=False), 0)
            step(qi, 0, True)
            res = acc_ref[...] / l_ref[...]
            if mp == 0:
                first = res
            else:
                for g in range(DIFF_GROUP):
                    a1 = first[g * tq:(g + 1) * tq]
                    a2 = res[g * tq:(g + 1) * tq]
                    hq = h * DIFF_GROUP + g
                    o_ref[:, hq * dv:(hq + 1) * dv] = _diff_head_out(a1, a2, lam, sub_ref[...])


def _diff_prompt(p, n_batch, seq, lam_vec, subln_g):
    n_in = p.shape[1]
    nq = DIFF_KV_HEADS * DIFF_GROUP * 2 * HEAD_DIM
    tq = math.gcd(seq, 256)
    p3 = p.reshape(n_batch, seq, n_in)
    o = pl.pallas_call(
        functools.partial(_diff_prompt_kernel, tq=tq),
        grid=(n_batch, seq // tq),
        in_specs=[
            pl.BlockSpec((None, tq, nq), lambda b, i: (b, i, 0)),
            pl.BlockSpec((None, seq, nq), lambda b, i: (b, 0, 1)),
            pl.BlockSpec((4, HEAD_DIM), lambda b, i: (0, 0)),
            pl.BlockSpec((1, 2 * HEAD_DIM), lambda b, i: (0, 0)),
        ],
        out_specs=pl.BlockSpec((None, tq, nq), lambda b, i: (b, i, 0)),
        out_shape=jax.ShapeDtypeStruct((n_batch, seq, nq), F32),
        scratch_shapes=[pltpu.VMEM((DIFF_GROUP * tq, 1), F32), pltpu.VMEM((DIFF_GROUP * tq, 1), F32),
                        pltpu.VMEM((DIFF_GROUP * tq, 2 * HEAD_DIM), F32)],
        compiler_params=_cp("parallel", "arbitrary"),
        name="diff_attn",
    )(p3, p3, lam_vec.astype(F32), subln_g.reshape(1, -1).astype(F32))
    return o.reshape(n_batch * seq, nq)


def _diff_sample_kernel(pt_ref, q_ref, kv_ref, lam_ref, sub_ref, *rest, pps):
    pages = rest[:pps]
    o_ref, qb_ref, m_ref, l_ref, acc_ref, nk_ref = rest[pps:]
    step = pl.program_id(1)
    n_step = pl.num_programs(1)
    s_len = q_ref.shape[0]
    dh = HEAD_DIM
    dv = 2 * dh
    kw = DIFF_KV_HEADS * dv
    n_rows = DIFF_KV_HEADS * DIFF_GROUP * 2 * s_len

    @pl.when(step == 0)
    def _():
        qb_ref[...] = jnp.zeros_like(qb_ref)
        for h in range(DIFF_KV_HEADS):
            for g in range(DIFF_GROUP):
                for mp in range(2):
                    idx = (h * DIFF_GROUP + g) * 2 + mp
                    qb_ref[idx * s_len:(idx + 1) * s_len, h * dv + mp * dh:h * dv + (mp + 1) * dh] = (
                        q_ref[:, idx * dh:(idx + 1) * dh])
        m_ref[...] = jnp.full_like(m_ref, NEG)
        l_ref[...] = jnp.zeros_like(l_ref)
        acc_ref[...] = jnp.zeros_like(acc_ref)

    qb = qb_ref[...].astype(BF16)

    def update(s, v_list):
        m_old = m_ref[...]
        m_new = m_old
        for t in s:
            m_new = jnp.maximum(m_new, jnp.max(t, axis=-1, keepdims=True))
        corr = jnp.exp(m_old - m_new)
        l_new = l_ref[...] * corr
        acc = acc_ref[...] * corr
        for t, v in zip(s, v_list):
            p = jnp.exp(t - m_new)
            l_new = l_new + jnp.sum(p, axis=-1, keepdims=True)
            acc = acc + _dot(p.astype(v.dtype), v)
        m_ref[...] = m_new
        l_ref[...] = l_new
        acc_ref[...] = acc

    update([_dot_nt(qb, pg[:, :kw].astype(BF16)) for pg in pages],
           [pg[:, kw:].astype(BF16) for pg in pages])

    @pl.when(step == n_step - 1)
    def _():
        nk_ref[...] = jnp.zeros_like(nk_ref)
        nk_ref[:s_len, :] = kv_ref[...]
        r_s = lax.broadcasted_iota(jnp.int32, (n_rows, PAGE_SIZE), 0) % s_len
        c_s = lax.broadcasted_iota(jnp.int32, (n_rows, PAGE_SIZE), 1)
        s_new = jnp.where(c_s <= r_s, _dot_nt(qb, nk_ref[:, :kw].astype(BF16)), NEG)
        update([s_new], [nk_ref[:, kw:].astype(BF16)])
        lam = _diff_lambda(lam_ref)
        res = acc_ref[...] / l_ref[...]
        for h in range(DIFF_KV_HEADS):
            for g in range(DIFF_GROUP):
                hq = h * DIFF_GROUP + g
                a1 = res[(hq * 2) * s_len:(hq * 2 + 1) * s_len, h * dv:(h + 1) * dv]
                a2 = res[(hq * 2 + 1) * s_len:(hq * 2 + 2) * s_len, h * dv:(h + 1) * dv]
                o_ref[:, hq * dv:(hq + 1) * dv] = _diff_head_out(a1, a2, lam, sub_ref[...])


def _diff_sample(p, n_batch, s_len, cache, page_table, lam_vec, subln_g):
    n_in = p.shape[1]
    nq = DIFF_KV_HEADS * DIFF_GROUP * 2 * HEAD_DIM
    n_pages = page_table.shape[1]
    pps = math.gcd(n_pages, 8)
    n_rows = DIFF_KV_HEADS * DIFF_GROUP * 2 * s_len
    p3 = p.reshape(n_batch, s_len, n_in)
    kvw = 2 * DIFF_KV_HEADS * 2 * HEAD_DIM
    cache3 = cache.reshape(cache.shape[0], PAGE_SIZE, kvw)

    def page_spec(jj):
        return pl.BlockSpec((None, PAGE_SIZE, kvw), lambda b, s, pt: (pt[b * n_pages + s * pps + jj], 0, 0))

    grid_spec = pltpu.PrefetchScalarGridSpec(
        num_scalar_prefetch=1,
        grid=(n_batch, n_pages // pps),
        in_specs=[
            pl.BlockSpec((None, s_len, nq), lambda b, s, pt: (b, 0, 0)),
            pl.BlockSpec((None, s_len, nq), lambda b, s, pt: (b, 0, 1)),
            pl.BlockSpec((4, HEAD_DIM), lambda b, s, pt: (0, 0)),
            pl.BlockSpec((1, 2 * HEAD_DIM), lambda b, s, pt: (0, 0)),
        ] + [page_spec(jj) for jj in range(pps)],
        out_specs=pl.BlockSpec((None, s_len, nq), lambda b, s, pt: (b, 0, 0)),
        scratch_shapes=[pltpu.VMEM((n_rows, kvw // 2), F32), pltpu.VMEM((n_rows, 1), F32),
                        pltpu.VMEM((n_rows, 1), F32), pltpu.VMEM((n_rows, kvw // 2), F32),
                        pltpu.VMEM((PAGE_SIZE, kvw), F32)],
    )
    o = pl.pallas_call(
        functools.partial(_diff_sample_kernel, pps=pps),
        grid_spec=grid_spec,
        out_shape=jax.ShapeDtypeStruct((n_batch, s_len, nq), F32),
        compiler_params=_cp("parallel", "arbitrary"),
        name="diff_decode",
    )(page_table.reshape(-1), p3, p3, lam_vec.astype(F32), subln_g.reshape(1, -1).astype(F32),
      *([cache3] * pps))
    return o.reshape(n_batch * s_len, nq)


def kernel(x_prompt, x_sample, state_gla, cache_swa, cache_dil1, cache_dil2, cache_dil3, cache_diff, page_table, c_prompt, c_sample, norm1_g, norm2_g, ada_w, ada_b, gla_w_in, gla_w_gate2, gla_b_gate, gla_onorm_g, gla_w_out, swa_w_in, swa_q_norm, swa_k_norm, swa_sinks, swa_w_out, dil_w_in, dil_q_norm, dil_k_norm, dil_w_out, diff_w_in, diff_q_norm, diff_k_norm, diff_lambda, diff_subln_g, diff_w_out, ffn_w_gu, ffn_w_down, moe_router, moe_w_gu, moe_w_down):
    bp, seq, d = x_prompt.shape
    db, s_len, _ = x_sample.shape
    depth = ada_w.shape[0]
    dh = HEAD_DIM
    qscale = dh ** -0.5

    mods = _ada(jnp.concatenate([c_prompt, c_sample], axis=0), ada_w, ada_b)
    xp = x_prompt.reshape(bp * seq, d)
    xs = x_sample.reshape(db * s_len, d)
    out = {}

    for i in range(depth):
        gp = _Group(bp, seq, mods[i, :bp])
        gs = _Group(db, s_len, mods[i, bp:])
        kind = i % 4
        if kind == 0:
            n_in = gla_w_in.shape[1]
            n_pad = -(-n_in // (5 * LANES)) * (5 * LANES)
            w_in = jnp.pad(gla_w_in, ((0, 0), (0, n_pad - n_in)))
            lr_w = n_pad - (n_in - GLA_RANK)
            wg2 = jnp.pad(gla_w_gate2, ((0, lr_w - GLA_RANK), (0, 0)))
            pp = _inproj(gp, xp, norm1_g[i], 0, w_in, n_pad // 5)
            ps = _inproj(gs, xs, norm1_g[i], 0, w_in, n_pad // 5)
            op, out["gla_p"] = _gla(pp, bp, seq, wg2, gla_b_gate, gla_onorm_g, None)
            os_, out["gla_s"] = _gla(ps, db, s_len, wg2, gla_b_gate, gla_onorm_g, state_gla)
            xp = _outproj(gp, xp, op, gla_w_out, 2)
            xs = _outproj(gs, xs, os_, gla_w_out, 2)
        elif kind == 1:
            hk = SWA_KV_HEADS
            nq = swa_w_out.shape[0]
            grp = nq // dh // hk
            kvw = hk * dh
            norm = _norm_rows([(nq, swa_q_norm, qscale), (kvw, swa_k_norm, 1.0), (kvw, None, 1.0)])
            pp = _inproj(gp, xp, norm1_g[i], 0, swa_w_in, kvw * 2, norm)
            ps = _inproj(gs, xs, norm1_g[i], 0, swa_w_in, kvw * 2, norm)
            (op,) = _banded(pp, bp, seq, 1, (nq, 0), (kvw, nq // kvw), (kvw, nq // kvw + 1),
                            hk, grp, SWA_WINDOW, sinks=swa_sinks)
            keep = min(SWA_WINDOW, seq)
            out["swa_p"] = pp.reshape(bp, seq, -1)[:, seq - keep:, nq:].reshape(bp, keep, 2, hk, dh)
            os_, out["swa_s"] = _swa_sample(ps, db, s_len, cache_swa, swa_sinks, hk, grp)
            xp = _outproj(gp, xp, op, swa_w_out, 2)
            xs = _outproj(gs, xs, os_, swa_w_out, 2)
        elif kind == 2:
            w = DIL_HEADS * dh
            norm = _norm_rows([(w, dil_q_norm, qscale), (w, dil_k_norm, 1.0), (w, None, 1.0)] * len(DIL_GROUPS))
            pp = _inproj(gp, xp, norm1_g[i], 0, dil_w_in, w, norm)
            ps = _inproj(gs, xs, norm1_g[i], 0, dil_w_in, w, norm)
            outs_p, lses_p, outs_s, lses_s = [], [], [], []
            caches = (cache_dil1, cache_dil2, cache_dil3)
            for gi, (win, dil) in enumerate(DIL_GROUPS):
                o, lse = _banded(pp, bp, seq, dil, (w, 3 * gi), (w, 3 * gi + 1), (w, 3 * gi + 2),
                                 DIL_HEADS, 1, win // dil, want_lse=True)
                outs_p.append(o)
                lses_p.append(lse)
                keep = min(win, seq)
                c0 = (3 * gi + 1) * w
                out["dil%d_p" % gi] = pp.reshape(bp, seq, -1)[:, seq - keep:, c0:c0 + 2 * w].reshape(
                    bp, keep, 2, DIL_HEADS, dh)
                o, lse, out["dil%d_s" % gi] = _dil_sample(ps, db, s_len, caches[gi], gi, win, dil)
                outs_s.append(o)
                lses_s.append(lse)
            xp = _outproj_mix(gp, xp, outs_p, lses_p, dil_w_out, 2)
            xs = _outproj_mix(gs, xs, outs_s, lses_s, dil_w_out, 2)
        else:
            nq = DIFF_KV_HEADS * DIFF_GROUP * 2 * dh
            nk = DIFF_KV_HEADS * 2 * dh
            norm = _norm_rows([(nq, diff_q_norm, qscale), (nk, diff_k_norm, 1.0), (nk, None, 1.0)])
            pp = _inproj(gp, xp, norm1_g[i], 0, diff_w_in, nk, norm)
            ps = _inproj(gs, xs, norm1_g[i], 0, diff_w_in, nk, norm)
            op = _diff_prompt(pp, bp, seq, diff_lambda, diff_subln_g)
            os_ = _diff_sample(ps, db, s_len, cache_diff, page_table, diff_lambda, diff_subln_g)
            out["diff_p"] = pp[:, nq:].reshape(bp, seq, 2, DIFF_KV_HEADS, 2 * dh)
            out["diff_s"] = ps[:, nq:].reshape(db, s_len, 2, DIFF_KV_HEADS, 2 * dh)
            xp = _outproj(gp, xp, op, diff_w_out, 2)
            xs = _outproj(gs, xs, os_, diff_w_out, 2)
        j = i // 2
        if i % 2 == 0:
            xp = _ffn(gp, xp, norm2_g[i], ffn_w_gu[j], ffn_w_down[j], 256)
            xs = _ffn(gs, xs, norm2_g[i], ffn_w_gu[j], ffn_w_down[j], 256)
        else:
            xp = _moe(gp, xp, norm2_g[i], moe_router[j], moe_w_gu[j], moe_w_down[j], 512)
            xs = _moe(gs, xs, norm2_g[i], moe_router[j], moe_w_gu[j], moe_w_down[j], 512)

    return (xp.reshape(bp, seq, d), xs.reshape(db, s_len, d), out["gla_p"], out["gla_s"],
            out["swa_p"], out["swa_s"], out["dil0_p"], out["dil0_s"], out["dil1_p"], out["dil1_s"],
            out["dil2_p"], out["dil2_s"], out["diff_p"], out["diff_s"])
```

```python
import functools
import math

import jax
import jax.numpy as jnp
from jax import lax
from jax.experimental import pallas as pl
from jax.experimental.pallas import tpu as pltpu

F32 = jnp.float32
BF16 = jnp.bfloat16

EPS = 1e-6
NEG = -1e30
BLK = 128
HEAD_DIM = 64

GLA_HEADS = 4
GLA_RANK = 16
GLA_NORMALIZER = 16.0
GLA_CHUNK = 64

SWA_KV_HEADS = 4
SWA_WINDOW = 128
DIL_GROUPS = ((128, 1), (512, 4), (2048, 16))
DIL_HEADS = 8
DIFF_KV_HEADS = 4
DIFF_GROUP = 2
LAMBDA_INIT = 0.8 - 0.6 * math.exp(-0.3 * 3)
N_EXPERTS = 8
PAGE_SIZE = 128

LANES = 128
VMEM_LIMIT = 56 * 1024 * 1024


def _cp(*sem):
    return pltpu.CompilerParams(dimension_semantics=sem, vmem_limit_bytes=VMEM_LIMIT)


def _dot(a, b):
    return jnp.dot(a, b, preferred_element_type=F32)


def _dot_nt(a, b):
    return lax.dot_general(a, b, (((1,), (1,)), ((), ())), preferred_element_type=F32)


def _dot_tn(a, b):
    return lax.dot_general(a, b, (((0,), (0,)), ((), ())), preferred_element_type=F32)


def _silu(x):
    return x / (1.0 + jnp.exp(-x))


def _adaln(x, g, shift, scale):
    y = x * lax.rsqrt(jnp.mean(x * x, axis=-1, keepdims=True) + EPS)
    return y * g * (1.0 + scale) + shift


class _Group:
    def __init__(self, n_batch, rows_per_batch, mods):
        self.m = n_batch * rows_per_batch
        self.rows_per_batch = rows_per_batch
        self.prompt = rows_per_batch >= 256
        d6 = mods.shape[-1]
        if self.prompt:
            self.mods = mods.reshape(n_batch, 1, d6)
        else:
            self.mods = jnp.repeat(mods, rows_per_batch, axis=0)
        self.d = d6 // 6
        self.tm = self.tile(1024)

    def tile(self, tm_max):
        return math.gcd(self.rows_per_batch if self.prompt else self.m, tm_max)

    def mod_spec(self, k, tm=None):
        d = self.d
        tm = tm or self.tm
        if self.prompt:
            tpb = self.rows_per_batch // tm
            return pl.BlockSpec((None, 1, d), lambda i, *_: (i // tpb, 0, k))
        return pl.BlockSpec((tm, d), lambda i, *_: (i, k))


def _ada_kernel(c_ref, w_ref, b_ref, o_ref):
    a = _silu(c_ref[...]).astype(BF16)
    o_ref[...] = _dot(a, w_ref[...].astype(BF16)) + b_ref[...]


def _ada(c_all, ada_w, ada_b):
    depth, d, d6 = ada_w.shape
    nb = c_all.shape[0]
    tn = 1024
    return pl.pallas_call(
        _ada_kernel,
        grid=(depth, d6 // tn),
        in_specs=[
            pl.BlockSpec((nb, d), lambda l, j: (0, 0)),
            pl.BlockSpec((None, d, tn), lambda l, j: (l, 0, j)),
            pl.BlockSpec((None, 1, tn), lambda l, j: (l, 0, j)),
        ],
        out_specs=pl.BlockSpec((None, nb, tn), lambda l, j: (l, 0, j)),
        out_shape=jax.ShapeDtypeStruct((depth, nb, d6), F32),
        compiler_params=_cp("parallel", "parallel"),
        name="ada_mod",
    )(c_all, ada_w, ada_b.reshape(depth, 1, d6))


def _inproj_kernel(x_ref, g_ref, sh_ref, sc_ref, w_ref, *rest, norm):
    if norm:
        flag_ref, gain_ref, o_ref, h_ref = rest
    else:
        o_ref, h_ref = rest
    j = pl.program_id(1)

    @pl.when(j == 0)
    def _():
        h_ref[...] = _adaln(x_ref[...], g_ref[...], sh_ref[...], sc_ref[...]).astype(BF16)

    acc = _dot(h_ref[...], w_ref[...].astype(BF16))
    if not norm:
        o_ref[...] = acc
        return
    tn = acc.shape[1]
    lo = lax.broadcasted_iota(jnp.int32, (1, LANES), 1) < HEAD_DIM
    for c in range(tn // LANES):
        sl = slice(c * LANES, (c + 1) * LANES)
        a = acc[:, sl]
        sq = a * a
        s_lo = jnp.sum(jnp.where(lo, sq, 0.0), axis=-1, keepdims=True)
        s_hi = jnp.sum(jnp.where(lo, 0.0, sq), axis=-1, keepdims=True)
        r = lax.rsqrt(jnp.where(lo, s_lo, s_hi) * (1.0 / HEAD_DIM) + EPS)
        o_ref[:, sl] = a * jnp.where(flag_ref[:, sl] > 0.0, r, 1.0) * gain_ref[:, sl]


def _inproj(grp, x, norm_g, k_shift, w, tn, norm=None):
    m, d = x.shape
    n = w.shape[1]
    tm = grp.tm
    in_specs = [
        pl.BlockSpec((tm, d), lambda i, j: (i, 0)),
        pl.BlockSpec((1, d), lambda i, j: (0, 0)),
        grp.mod_spec(k_shift),
        grp.mod_spec(k_shift + 1),
        pl.BlockSpec((d, tn), lambda i, j: (0, j)),
    ]
    args = [x, norm_g.reshape(1, d), grp.mods, grp.mods, w]
    if norm is not None:
        in_specs += [pl.BlockSpec((1, tn), lambda i, j: (0, j))] * 2
        args += list(norm)
    return pl.pallas_call(
        functools.partial(_inproj_kernel, norm=norm is not None),
        grid=(m // tm, n // tn),
        in_specs=in_specs,
        out_specs=pl.BlockSpec((tm, tn), lambda i, j: (i, j)),
        out_shape=jax.ShapeDtypeStruct((m, n), F32),
        scratch_shapes=[pltpu.VMEM((tm, d), BF16)],
        compiler_params=_cp("parallel", "arbitrary"),
        name="adaln_inproj",
    )(*args)


def _norm_rows(pieces):
    flags, gains = [], []
    for width, g, scale in pieces:
        if g is None:
            flags.append(jnp.zeros((width,), F32))
            gains.append(jnp.ones((width,), F32))
        else:
            flags.append(jnp.ones((width,), F32))
            gains.append(jnp.tile(g.astype(F32) * scale, width // HEAD_DIM))
    return jnp.concatenate(flags)[None, :], jnp.concatenate(gains)[None, :]


def _outproj_kernel(x_ref, o_ref, gt_ref, w_ref, out_ref):
    y = _dot(o_ref[...].astype(BF16), w_ref[...].astype(BF16))
    out_ref[...] = x_ref[...] + gt_ref[...] * y


def _outproj(grp, x, o, w, k_gate):
    m, d = x.shape
    kdim = w.shape[0]
    tm = grp.tm
    return pl.pallas_call(
        _outproj_kernel,
        grid=(m // tm,),
        in_specs=[
            pl.BlockSpec((tm, d), lambda i: (i, 0)),
            pl.BlockSpec((tm, kdim), lambda i: (i, 0)),
            grp.mod_spec(k_gate),
            pl.BlockSpec((kdim, d), lambda i: (0, 0)),
        ],
        out_specs=pl.BlockSpec((tm, d), lambda i: (i, 0)),
        out_shape=jax.ShapeDtypeStruct((m, d), F32),
        compiler_params=_cp("parallel"),
        name="outproj",
    )(x, o, grp.mods, w)


def _outproj_mix_kernel(x_ref, o0_ref, o1_ref, o2_ref, l0_ref, l1_ref, l2_ref, gt_ref, w_ref, out_ref):
    l0, l1, l2 = l0_ref[...], l1_ref[...], l2_ref[...]
    mx = jnp.maximum(jnp.maximum(l0, l1), l2)
    e0, e1, e2 = jnp.exp(l0 - mx), jnp.exp(l1 - mx), jnp.exp(l2 - mx)
    o = (e0 * o0_ref[...] + e1 * o1_ref[...] + e2 * o2_ref[...]) / (e0 + e1 + e2)
    y = _dot(o.astype(BF16), w_ref[...].astype(BF16))
    out_ref[...] = x_ref[...] + gt_ref[...] * y


def _outproj_mix(grp, x, outs, lses, w, k_gate):
    m, d = x.shape
    kdim = w.shape[0]
    tm = grp.tm
    row = pl.BlockSpec((tm, kdim), lambda i: (i, 0))
    return pl.pallas_call(
        _outproj_mix_kernel,
        grid=(m // tm,),
        in_specs=[pl.BlockSpec((tm, d), lambda i: (i, 0))] + [row] * 6
        + [grp.mod_spec(k_gate), pl.BlockSpec((kdim, d), lambda i: (0, 0))],
        out_specs=pl.BlockSpec((tm, d), lambda i: (i, 0)),
        out_shape=jax.ShapeDtypeStruct((m, d), F32),
        compiler_params=_cp("parallel"),
        name="outproj_mix",
    )(x, *outs, *lses, grp.mods, w)


def _ffn_kernel(x_ref, g_ref, sh_ref, sc_ref, gt_ref, wg_ref, wu_ref, wd_ref, o_ref, h_ref):
    f = pl.program_id(1)
    nf = pl.num_programs(1)

    @pl.when(f == 0)
    def _():
        h_ref[...] = _adaln(x_ref[...], g_ref[...], sh_ref[...], sc_ref[...]).astype(BF16)
        o_ref[...] = jnp.zeros_like(o_ref)

    h = h_ref[...]
    gate = _dot(h, wg_ref[...].astype(BF16))
    up = _dot(h, wu_ref[...].astype(BF16))
    a = (_silu(gate) * up).astype(BF16)
    o_ref[...] += _dot(a, wd_ref[...].astype(BF16))

    @pl.when(f == nf - 1)
    def _():
        o_ref[...] = x_ref[...] + gt_ref[...] * o_ref[...]


def _ffn(grp, x, norm_g, w_gu, w_down, tf):
    m, d = x.shape
    dff = w_down.shape[0]
    tm = grp.tm
    nf = dff // tf
    return pl.pallas_call(
        _ffn_kernel,
        grid=(m // tm, nf),
        in_specs=[
            pl.BlockSpec((tm, d), lambda i, f: (i, 0)),
            pl.BlockSpec((1, d), lambda i, f: (0, 0)),
            grp.mod_spec(3), grp.mod_spec(4), grp.mod_spec(5),
            pl.BlockSpec((d, tf), lambda i, f: (0, f)),
            pl.BlockSpec((d, tf), lambda i, f: (0, f + nf)),
            pl.BlockSpec((tf, d), lambda i, f: (f, 0)),
        ],
        out_specs=pl.BlockSpec((tm, d), lambda i, f: (i, 0)),
        out_shape=jax.ShapeDtypeStruct((m, d), F32),
        scratch_shapes=[pltpu.VMEM((tm, d), BF16)],
        compiler_params=_cp("parallel", "arbitrary"),
        name="ffn_swiglu",
    )(x, norm_g.reshape(1, d), grp.mods, grp.mods, grp.mods, w_gu, w_gu, w_down)


def _route_top2(logits):
    lane = lax.broadcasted_iota(jnp.int32, logits.shape, 1)
    lg = jnp.where(lane < N_EXPERTS, logits, -jnp.inf)
    m1 = jnp.max(lg, axis=-1, keepdims=True)
    i1 = jnp.min(jnp.where(lg == m1, lane, LANES), axis=-1, keepdims=True)
    lg2 = jnp.where(lane == i1, -jnp.inf, lg)
    m2 = jnp.max(lg2, axis=-1, keepdims=True)
    i2 = jnp.min(jnp.where(lg2 == m2, lane, LANES), axis=-1, keepdims=True)
    e2 = jnp.exp(m2 - m1)
    g1 = 1.0 / (1.0 + e2)
    g2 = e2 / (1.0 + e2)
    comb = jnp.where(lane == i1, g1, 0.0) + jnp.where(lane == i2, g2, 0.0)
    return comb, (lane == i1) | (lane == i2)


def _moe_kernel(x_ref, g_ref, sh_ref, sc_ref, gt_ref, r_ref, wg_ref, wu_ref, wd_ref, o_ref,
                h_ref, comb_ref, key_ref, keyt_ref, cnt_ref, xs_ref, acc_ref, wgb_ref, wub_ref, wdb_ref, n_ref,
                *, sub):
    e = pl.program_id(1)
    f = pl.program_id(2)
    nf = pl.num_programs(2)
    tm = h_ref.shape[0]
    n_sub = tm // sub
    lane = lax.broadcasted_iota(jnp.int32, (tm, LANES), 1)

    @pl.when((e == 0) & (f == 0))
    def _():
        x = x_ref[...]
        h = _adaln(x, g_ref[...], sh_ref[...], sc_ref[...]).astype(BF16)
        h_ref[...] = h
        comb, sel = _route_top2(_dot(h, r_ref[...].astype(BF16)))
        comb_ref[...] = comb
        tb = math.gcd(tm, 256)
        tri = jnp.where(lax.broadcasted_iota(jnp.int32, (tb, tb), 0) >= lax.broadcasted_iota(jnp.int32, (tb, tb), 1),
                        1.0, 0.0).astype(BF16)
        carry = jnp.zeros((1, LANES), F32)
        for blk in range(tm // tb):
            rows = slice(blk * tb, (blk + 1) * tb)
            sb = jnp.where(sel[rows], 1.0, 0.0)
            incl = _dot(tri, sb.astype(BF16)) + carry
            key_ref[rows, :] = jnp.where(sb > 0.0, incl - 1.0, -1.0).astype(jnp.int32)
            carry = carry + jnp.sum(sb, axis=0, keepdims=True)
        cnt_ref[...] = carry
        keyt_ref[...] = key_ref[...].T
        o_ref[...] = x

    def block_onehot(s):
        keyrow = keyt_ref[pl.ds(e, 1), :]
        r = lax.broadcasted_iota(jnp.int32, (sub, tm), 0) + s * sub
        return jnp.where(keyrow == r, 1.0, 0.0).astype(BF16)

    @pl.when(f == 0)
    def _():
        cnt = jnp.sum(jnp.where(lane[:1] == e, cnt_ref[...], 0.0), axis=-1, keepdims=True)
        n_ref[0] = cnt.astype(jnp.int32)[0, 0]

    n_e = n_ref[0]

    @pl.when(f == 0)
    def _():
        for s in range(n_sub):
            @pl.when(s * sub < n_e)
            def _():
                xs_ref[s * sub:(s + 1) * sub, :] = _dot(block_onehot(s), h_ref[...]).astype(BF16)
                acc_ref[s * sub:(s + 1) * sub, :] = jnp.zeros((sub, acc_ref.shape[1]), F32)

    wgb_ref[...] = wg_ref[...].astype(BF16)
    wub_ref[...] = wu_ref[...].astype(BF16)
    wdb_ref[...] = wd_ref[...].astype(BF16)
    for s in range(n_sub):
        @pl.when(s * sub < n_e)
        def _():
            xb = xs_ref[s * sub:(s + 1) * sub, :]
            a = (_silu(_dot(xb, wgb_ref[...])) * _dot(xb, wub_ref[...])).astype(BF16)
            acc_ref[s * sub:(s + 1) * sub, :] += _dot(a, wdb_ref[...])

    @pl.when(f == nf - 1)
    def _():
        ce = jnp.sum(jnp.where(lane == e, comb_ref[...], 0.0), axis=-1, keepdims=True)
        c_hi = ce.astype(BF16).astype(F32)
        c_mid = (ce - c_hi).astype(BF16).astype(F32)
        c_lo = ce - c_hi - c_mid
        cw = jnp.where(lane == 0, c_hi, jnp.where(lane == 1, c_mid, jnp.where(lane == 2, c_lo, 0.0))).astype(BF16)
        kcol = jnp.sum(jnp.where(lane == e, key_ref[...].astype(F32), 0.0), axis=-1,
                       keepdims=True).astype(jnp.int32)
        for s in range(n_sub):
            @pl.when(s * sub < n_e)
            def _():
                g3 = _dot(block_onehot(s), cw)
                gate = g3[:, 0:1] + g3[:, 1:2] + g3[:, 2:3]
                y = (acc_ref[s * sub:(s + 1) * sub, :] * gate).astype(BF16)
                c = lax.broadcasted_iota(jnp.int32, (tm, sub), 1) + s * sub
                scatter = jnp.where(kcol == c, 1.0, 0.0).astype(BF16)
                o_ref[...] += gt_ref[...] * _dot(scatter, y)


def _moe(grp, x, norm_g, router, w_gu, w_down, tf):
    m, d = x.shape
    ne, dff, _ = w_down.shape
    tm = grp.tile(2048)
    sub = math.gcd(tm, 256)
    nf = dff // tf
    router_p = jnp.pad(router, ((0, 0), (0, LANES - ne)))
    resident = dict(pipeline_mode=pl.Buffered(1))
    return pl.pallas_call(
        functools.partial(_moe_kernel, sub=sub),
        grid=(m // tm, ne, nf),
        in_specs=[
            pl.BlockSpec((tm, d), lambda i, e, f: (i, 0), **resident),
            pl.BlockSpec((1, d), lambda i, e, f: (0, 0)),
            grp.mod_spec(3, tm), grp.mod_spec(4, tm), grp.mod_spec(5, tm),
            pl.BlockSpec((d, LANES), lambda i, e, f: (0, 0)),
            pl.BlockSpec((None, d, tf), lambda i, e, f: (e, 0, f)),
            pl.BlockSpec((None, d, tf), lambda i, e, f: (e, 0, f + nf)),
            pl.BlockSpec((None, tf, d), lambda i, e, f: (e, f, 0)),
        ],
        out_specs=pl.BlockSpec((tm, d), lambda i, e, f: (i, 0), **resident),
        out_shape=jax.ShapeDtypeStruct((m, d), F32),
        scratch_shapes=[
            pltpu.VMEM((tm, d), BF16), pltpu.VMEM((tm, LANES), F32), pltpu.VMEM((tm, LANES), jnp.int32),
            pltpu.VMEM((LANES, tm), jnp.int32), pltpu.VMEM((1, LANES), F32),
            pltpu.VMEM((tm, d), BF16), pltpu.VMEM((tm, d), F32),
            pltpu.VMEM((d, tf), BF16), pltpu.VMEM((d, tf), BF16), pltpu.VMEM((tf, d), BF16),
            pltpu.SMEM((1,), jnp.int32),
        ],
        compiler_params=_cp("parallel", "arbitrary", "arbitrary"),
        name="moe",
    )(x, norm_g.reshape(1, d), grp.mods, grp.mods, grp.mods, router_p, w_gu, w_gu, w_down)


def _gla_kernel(p_ref, wg2_ref, bg_ref, on_ref, *rest, chunk, n_chunk, n_valid, has_s0):
    rest = list(rest)
    s0_ref = rest.pop(0) if has_s0 else None
    o_ref, sfin_ref, s_ref = rest[:3]
    if n_valid < chunk:
        src = rest[3]
        src[...] = jnp.zeros_like(src)
        src[:n_valid, :] = p_ref[...]
    else:
        src = p_ref
    li = pl.program_id(1)
    nl = pl.num_programs(1)
    nq = GLA_HEADS * 128
    nv = GLA_HEADS * 256
    dk, dv = 128, 256

    @pl.when(li == 0)
    def _():
        if has_s0:
            s_ref[...] = s0_ref[...]
        else:
            s_ref[...] = jnp.zeros_like(s_ref)

    c = chunk
    row = lax.broadcasted_iota(jnp.int32, (c, c), 0)
    col = lax.broadcasted_iota(jnp.int32, (c, c), 1)
    causal = row >= col
    tri = jnp.where(causal, 1.0, 0.0).astype(BF16)
    ones = jnp.ones((c, dv), BF16)
    live = None
    if n_valid < c:
        live = lax.broadcasted_iota(jnp.int32, (c, 1), 0) < n_valid
    for ci in range(n_chunk):
        rows = slice(ci * c, (ci + 1) * c)
        a_lr = src[rows, 2 * nq + 2 * nv:]
        z = _dot(a_lr.astype(BF16), wg2_ref[...].astype(BF16)) + bg_ref[...]
        la = (jnp.minimum(z, 0.0) - jnp.log1p(jnp.exp(-jnp.abs(z)))) * (1.0 / GLA_NORMALIZER)
        if live is not None:
            la = jnp.where(live, la, 0.0)
        hi = la.astype(BF16)
        lo = (la - hi.astype(F32)).astype(BF16)
        b = _dot(tri, hi) + _dot(tri, lo)
        for h in range(GLA_HEADS):
            ks = slice(h * dk, (h + 1) * dk)
            vs = slice(h * dv, (h + 1) * dv)
            bh = b[:, ks]
            bl = bh[c - 1:c, :]
            qh = src[rows, h * dk:(h + 1) * dk] * (dk ** -0.5)
            kh = src[rows, nq + h * dk:nq + (h + 1) * dk]
            vh = src[rows, 2 * nq + h * dv:2 * nq + (h + 1) * dv]
            gh = src[rows, 2 * nq + nv + h * dv:2 * nq + nv + (h + 1) * dv]
            vb = vh.astype(BF16)
            qd = (qh * jnp.exp(bh)).astype(BF16)
            ki = (kh * jnp.exp(-bh)).astype(BF16)
            kd = (kh * jnp.exp(bl - bh)).astype(BF16)
            att = jnp.where(causal, _dot_nt(qd, ki), 0.0).astype(BF16)
            s = s_ref[h]
            o = _dot(att, vb) + _dot(qd, s.astype(BF16))
            dl = _dot_tn(hi[:, ks], ones) + _dot_tn(lo[:, ks], ones)
            s_ref[h] = jnp.exp(dl) * s + _dot_tn(kd, vb)
            on = o * lax.rsqrt(jnp.mean(o * o, axis=-1, keepdims=True) + EPS) * on_ref[...]
            res = on * _silu(gh)
            if n_valid < c:
                o_ref[:, vs] = res[:n_valid]
            else:
                o_ref[rows, vs] = res

    @pl.when(li == nl - 1)
    def _():
        sfin_ref[...] = s_ref[...]


def _gla(p, n_batch, seq, w_gate2_p, b_gate, onorm_g, s0):
    n_in = p.shape[1]
    nv = GLA_HEADS * 256
    c_ref = math.gcd(seq, GLA_CHUNK)
    if seq >= GLA_CHUNK:
        chunk, rows, n_valid = c_ref, math.gcd(seq, 256), c_ref
        scratch = []
    else:
        chunk, rows, n_valid = GLA_CHUNK, seq, seq
        scratch = [pltpu.VMEM((chunk, n_in), F32)]
    n_l = seq // rows
    p3 = p.reshape(n_batch, seq, n_in)
    has_s0 = s0 is not None
    st_spec = pl.BlockSpec((None, GLA_HEADS, 128, 256), lambda b, l: (b, 0, 0, 0))
    in_specs = [
        pl.BlockSpec((None, rows, n_in), lambda b, l: (b, l, 0)),
        pl.BlockSpec((LANES, GLA_HEADS * 128), lambda b, l: (0, 0)),
        pl.BlockSpec((1, GLA_HEADS * 128), lambda b, l: (0, 0)),
        pl.BlockSpec((1, 256), lambda b, l: (0, 0)),
    ]
    args = [p3, w_gate2_p, b_gate.reshape(1, -1), onorm_g.reshape(1, -1)]
    if has_s0:
        in_specs.append(st_spec)
        args.append(s0)
    o, s_fin = pl.pallas_call(
        functools.partial(_gla_kernel, chunk=chunk, n_chunk=max(rows // chunk, 1), n_valid=n_valid,
                          has_s0=has_s0),
        grid=(n_batch, n_l),
        in_specs=in_specs,
        out_specs=[pl.BlockSpec((None, rows, nv), lambda b, l: (b, l, 0)), st_spec],
        out_shape=[jax.ShapeDtypeStruct((n_batch, seq, nv), F32),
                   jax.ShapeDtypeStruct((n_batch, GLA_HEADS, 128, 256), F32)],
        scratch_shapes=[pltpu.VMEM((GLA_HEADS, 128, 256), F32)] + scratch,
        compiler_params=_cp("parallel", "arbitrary"),
        name="gla",
    )(*args)
    return o.reshape(n_batch * seq, nv), s_fin


def _banded_kernel(q_ref, kc_ref, kp_ref, vc_ref, vp_ref, *rest, hk, grp, dil, window, has_sink, want_lse):
    rest = list(rest)
    sink_ref = rest.pop(0) if has_sink else None
    o_ref = rest.pop(0)
    lse_ref = rest.pop(0) if want_lse else None
    n = pl.program_id(1)
    i = lax.broadcasted_iota(jnp.int32, (BLK, BLK), 0)
    j = lax.broadcasted_iota(jnp.int32, (BLK, BLK), 1)
    valid_c = (j <= i) & (i - j <= window)
    valid_p = i - j + BLK + jnp.where(n > 0, 0, 2 * BLK) <= window
    dh = HEAD_DIM
    for r in range(dil):
        rows = pl.ds(r, BLK, stride=dil) if dil > 1 else slice(None)
        q_all = q_ref[rows, :]
        kc_all, kp_all, vc_all, vp_all = kc_ref[rows, :], kp_ref[rows, :], vc_ref[rows, :], vp_ref[rows, :]
        outs, lses = [], []
        for h in range(hk):
            hs = slice(h * dh, (h + 1) * dh)
            kc = kc_all[:, hs].astype(BF16)
            kp = kp_all[:, hs].astype(BF16)
            vc = vc_all[:, hs].astype(BF16)
            vp = vp_all[:, hs].astype(BF16)
            for g in range(grp):
                qi = h * grp + g
                q = q_all[:, qi * dh:(qi + 1) * dh].astype(BF16)
                sc = jnp.where(valid_c, _dot_nt(q, kc), NEG)
                sp = jnp.where(valid_p, _dot_nt(q, kp), NEG)
                m = jnp.maximum(jnp.max(sc, axis=-1, keepdims=True), jnp.max(sp, axis=-1, keepdims=True))
                if has_sink:
                    sk = sink_ref[:, qi:qi + 1]
                    m = jnp.maximum(m, sk)
                pc = jnp.exp(sc - m)
                pp = jnp.exp(sp - m)
                l = jnp.sum(pc, axis=-1, keepdims=True) + jnp.sum(pp, axis=-1, keepdims=True)
                if has_sink:
                    l = l + jnp.exp(sk - m)
                outs.append((_dot(pc.astype(BF16), vc) + _dot(pp.astype(BF16), vp)) / l)
                if want_lse:
                    lses.append(jnp.broadcast_to(m + jnp.log(l), (BLK, dh)))
        o_ref[rows, :] = jnp.concatenate(outs, axis=1)
        if want_lse:
            lse_ref[rows, :] = jnp.concatenate(lses, axis=1)


def _banded(p, n_batch, seq, dil, q_blk, k_blk, v_blk, hk, grp, window, sinks=None, want_lse=False):
    n_in = p.shape[1]
    rows = dil * BLK
    assert seq % rows == 0
    nb = seq // rows
    p3 = p.reshape(n_batch, seq, n_in)
    qw, qi = q_blk
    kw, ki = k_blk
    vw, vi = v_blk
    if dil > 1:
        assert grp == 1 and qw == kw == vw and qw % LANES == 0
        n_cb = qw // LANES
        qi, ki, vi = qi * n_cb, ki * n_cb, vi * n_cb
        qw = kw = vw = LANES
        hk = LANES // HEAD_DIM
    else:
        n_cb = 1

    def cur(w, ci):
        return pl.BlockSpec((None, rows, w), lambda b, n, c: (b, n, ci + c))

    def prev(w, ci):
        return pl.BlockSpec((None, rows, w), lambda b, n, c: (b, jnp.maximum(n - 1, 0), ci + c))

    in_specs = [cur(qw, qi), cur(kw, ki), prev(kw, ki), cur(vw, vi), prev(vw, vi)]
    args = [p3] * 5
    if sinks is not None:
        in_specs.append(pl.BlockSpec((1, sinks.shape[0]), lambda b, n, c: (0, 0)))
        args.append(sinks.reshape(1, -1).astype(F32))
    o_spec = pl.BlockSpec((None, rows, qw), lambda b, n, c: (b, n, c))
    o_shape = jax.ShapeDtypeStruct((n_batch, seq, n_cb * qw), F32)
    outs = pl.pallas_call(
        functools.partial(_banded_kernel, hk=hk, grp=grp, dil=dil, window=window,
                          has_sink=sinks is not None, want_lse=want_lse),
        grid=(n_batch, nb, n_cb),
        in_specs=in_specs,
        out_specs=[o_spec, o_spec] if want_lse else [o_spec],
        out_shape=[o_shape, o_shape] if want_lse else [o_shape],
        compiler_params=_cp("parallel", "arbitrary", "arbitrary"),
        name="banded_attn",
    )(*args)
    return [t.reshape(n_batch * seq, n_cb * qw) for t in outs]


def _win_decode_kernel(q_ref, kn_ref, vn_ref, c_ref, *rest, bb, hk, grp, dil, win, has_sink, want_lse):
    rest = list(rest)
    sink_ref = rest.pop(0) if has_sink else None
    o_ref = rest.pop(0)
    lse_ref = rest.pop(0) if want_lse else None
    nc_ref, pad_ref = rest
    s_len = q_ref.shape[1]
    lb = c_ref.shape[2]
    dh = HEAD_DIM
    kvw = hk * dh
    n_row = grp * s_len
    new0 = LANES - s_len
    s_c = lax.broadcasted_iota(jnp.int32, (n_row, lb), 0) % s_len
    dist_c = lb + s_c - lax.broadcasted_iota(jnp.int32, (n_row, lb), 1)
    valid_c = (dist_c <= win) & ((dist_c & (dil - 1)) == 0)
    s_n = lax.broadcasted_iota(jnp.int32, (n_row, LANES), 0) % s_len
    c_n = lax.broadcasted_iota(jnp.int32, (n_row, LANES), 1) - new0
    dist_n = s_n - c_n
    valid_n = (c_n >= 0) & (dist_n >= 0) & (dist_n <= win) & ((dist_n & (dil - 1)) == 0)
    is_new = lax.broadcasted_iota(jnp.int32, (dh, LANES), 1) >= new0
    pad_ref[:new0, :] = jnp.zeros((new0, 2 * kvw), F32)
    for b in range(bb):
        pad_ref[new0:, :kvw] = kn_ref[b]
        pad_ref[new0:, kvw:] = vn_ref[b]
        new_t = pad_ref[...].T
        for h in range(hk):
            k_rows = slice(h * dh, (h + 1) * dh)
            v_rows = slice(kvw + h * dh, kvw + (h + 1) * dh)
            k_t = c_ref[b, k_rows, :]
            v_t = c_ref[b, v_rows, :]
            kn_t = new_t[k_rows]
            vn_t = new_t[v_rows]
            q = jnp.concatenate([q_ref[b, :, (h * grp + g) * dh:(h * grp + g + 1) * dh] for g in range(grp)],
                                axis=0)
            sc = jnp.where(valid_c, _dot(q.astype(BF16), k_t.astype(BF16)), NEG)
            sn = jnp.where(valid_n, _dot(q, kn_t), NEG)
            m = jnp.maximum(jnp.max(sc, axis=-1, keepdims=True), jnp.max(sn, axis=-1, keepdims=True))
            if has_sink:
                sk = jnp.concatenate(
                    [jnp.broadcast_to(sink_ref[:, h * grp + g:h * grp + g + 1], (s_len, 1)) for g in range(grp)],
                    axis=0)
                m = jnp.maximum(m, sk)
            pc = jnp.exp(sc - m)
            pn = jnp.exp(sn - m)
            l = jnp.sum(pc, axis=-1, keepdims=True) + jnp.sum(pn, axis=-1, keepdims=True)
            if has_sink:
                l = l + jnp.exp(sk - m)
            o = (_dot_nt(pc.astype(BF16), v_t.astype(BF16)) + _dot_nt(pn, vn_t)) / l
            for g in range(grp):
                qs = slice((h * grp + g) * dh, (h * grp + g + 1) * dh)
                o_ref[b, :, qs] = o[g * s_len:(g + 1) * s_len]
                if want_lse:
                    lse_ref[b, :, qs] = jnp.broadcast_to((m + jnp.log(l))[g * s_len:(g + 1) * s_len], (s_len, dh))
            for rows, x_t, x_new in ((k_rows, k_t, kn_t), (v_rows, v_t, vn_t)):
                moved = pltpu.roll(x_t, lb - s_len, axis=1)
                nc_ref[b, rows, :] = moved
                nc_ref[b, rows, lb - LANES:] = jnp.where(is_new, x_new, moved[:, lb - LANES:])


def _win_decode(p, n_batch, s_len, buf, q_blk, k_blk, v_blk, hk, grp, dil, win, sinks=None, want_lse=False):
    n_in = p.shape[1]
    lb = buf.shape[1]
    assert lb == win and lb % LANES == 0 and dil & (dil - 1) == 0 and s_len <= LANES
    dh = HEAD_DIM
    qw, kvw = hk * grp * dh, hk * dh
    buf_t = jnp.transpose(buf, (0, 2, 3, 4, 1)).reshape(n_batch, 2 * kvw, lb)
    bb = max(1, min(math.gcd(n_batch, 8), (2 * 1024 * 1024) // (lb * 2 * kvw * 4)))
    p3 = p.reshape(n_batch, s_len, n_in)
    (qwid, qi), (kwid, ki), (vwid, vi) = q_blk, k_blk, v_blk
    assert qwid == qw and kwid == kvw and vwid == kvw
    q_spec = pl.BlockSpec((bb, s_len, qw), lambda i: (i, 0, 0))
    buf_spec = pl.BlockSpec((bb, 2 * kvw, lb), lambda i: (i, 0, 0))
    in_specs = [
        pl.BlockSpec((bb, s_len, qw), lambda i: (i, 0, qi)),
        pl.BlockSpec((bb, s_len, kvw), lambda i: (i, 0, ki)),
        pl.BlockSpec((bb, s_len, kvw), lambda i: (i, 0, vi)),
        buf_spec,
    ]
    args = [p3, p3, p3, buf_t]
    if sinks is not None:
        in_specs.append(pl.BlockSpec((1, hk * grp), lambda i: (0, 0)))
        args.append(sinks.reshape(1, -1).astype(F32))
    q_shape = jax.ShapeDtypeStruct((n_batch, s_len, qw), F32)
    outs = pl.pallas_call(
        functools.partial(_win_decode_kernel, bb=bb, hk=hk, grp=grp, dil=dil, win=win,
                          has_sink=sinks is not None, want_lse=want_lse),
        grid=(n_batch // bb,),
        in_specs=in_specs,
        out_specs=[q_spec] * (2 if want_lse else 1) + [buf_spec],
        out_shape=[q_shape] * (2 if want_lse else 1) + [jax.ShapeDtypeStruct(buf_t.shape, F32)],
        scratch_shapes=[pltpu.VMEM((LANES, 2 * kvw), F32)],
        compiler_params=_cp("parallel"),
        name="win_decode",
    )(*args)
    nbuf = jnp.transpose(outs[-1].reshape(n_batch, 2, hk, dh, lb), (0, 4, 1, 2, 3))
    return [t.reshape(n_batch * s_len, qw) for t in outs[:-1]] + [nbuf]


def _diff_lambda(lam_ref):
    lam = lam_ref[...]
    a = jnp.sum(lam[0:1] * lam[1:2], axis=-1, keepdims=True)
    b = jnp.sum(lam[2:3] * lam[3:4], axis=-1, keepdims=True)
    return jnp.exp(a) - jnp.exp(b) + LAMBDA_INIT


def _diff_head_out(a1, a2, lam, sub_g):
    o = a1 - lam * a2
    return o * lax.rsqrt(jnp.mean(o * o, axis=-1, keepdims=True) + EPS) * sub_g * (1.0 - LAMBDA_INIT)


def _diff_prompt_kernel(q_ref, kv_ref, lam_ref, sub_ref, o_ref, qb_ref, m_ref, l_ref, acc_ref, *, tq):
    qi = pl.program_id(1)
    dh = HEAD_DIM
    dv = 2 * dh
    kw = DIFF_KV_HEADS * dv
    rows2 = DIFF_GROUP * tq
    r_i = lax.broadcasted_iota(jnp.int32, (rows2, tq), 0) % tq
    c_i = lax.broadcasted_iota(jnp.int32, (rows2, tq), 1)
    diag = c_i <= r_i
    lam = _diff_lambda(lam_ref)
    for h in range(DIFF_KV_HEADS):
        for mp in range(2):
            qb_ref[h * 2 + mp] = jnp.concatenate(
                [q_ref[:, ((h * DIFF_GROUP + g) * 2 + mp) * dh:((h * DIFF_GROUP + g) * 2 + mp + 1) * dh]
                 for g in range(DIFF_GROUP)], axis=0).astype(BF16)
    m_ref[...] = jnp.full_like(m_ref, NEG)
    l_ref[...] = jnp.zeros_like(l_ref)
    acc_ref[...] = jnp.zeros_like(acc_ref)

    def step(kb, carry, masked):
        r0 = pl.multiple_of(kb * tq, tq)
        for h in range(DIFF_KV_HEADS):
            v = kv_ref[pl.ds(r0, tq), kw + h * dv:kw + (h + 1) * dv].astype(BF16)
            for mp in range(2):
                c = h * 2 + mp
                k = kv_ref[pl.ds(r0, tq), h * dv + mp * dh:h * dv + (mp + 1) * dh].astype(BF16)
                s = _dot_nt(qb_ref[c], k)
                if masked:
                    s = jnp.where(diag, s, NEG)
                parts = [s[:, i * LANES:(i + 1) * LANES] for i in range(tq // LANES)]
                mx = parts[0]
                for t in parts[1:]:
                    mx = jnp.maximum(mx, t)
                m_old = m_ref[c]
                m_new = jnp.maximum(m_old, jnp.max(mx, axis=-1, keepdims=True))
                corr = jnp.exp(m_old - m_new)
                ps = [jnp.exp(t - m_new) for t in parts]
                l_new = l_ref[c] * corr
                for t in ps:
                    l_new = l_new + t
                l_ref[c] = l_new
                p = jnp.concatenate(ps, axis=1).astype(BF16)
                acc_ref[c] = acc_ref[c] * corr + _dot(p, v)
                m_ref[c] = m_new
        return carry

    lax.fori_loop(0, qi, functools.partial(step, masked=False), 0)
    step(qi, 0, True)
    for h in range(DIFF_KV_HEADS):
        res = [acc_ref[h * 2 + mp] / jnp.sum(l_ref[h * 2 + mp], axis=-1, keepdims=True) for mp in range(2)]
        for g in range(DIFF_GROUP):
            hq = h * DIFF_GROUP + g
            o_ref[:, hq * dv:(hq + 1) * dv] = _diff_head_out(res[0][g * tq:(g + 1) * tq], res[1][g * tq:(g + 1) * tq],
                                                             lam, sub_ref[...])


def _diff_prompt(p, n_batch, seq, lam_vec, subln_g):
    n_in = p.shape[1]
    nq = DIFF_KV_HEADS * DIFF_GROUP * 2 * HEAD_DIM
    tq = math.gcd(seq, 256)
    n_map = DIFF_KV_HEADS * 2
    assert 2 * HEAD_DIM == LANES and tq % LANES == 0
    p3 = p.reshape(n_batch, seq, n_in)
    o = pl.pallas_call(
        functools.partial(_diff_prompt_kernel, tq=tq),
        grid=(n_batch, seq // tq),
        in_specs=[
            pl.BlockSpec((None, tq, nq), lambda b, i: (b, i, 0)),
            pl.BlockSpec((None, seq, nq), lambda b, i: (b, 0, 1)),
            pl.BlockSpec((4, HEAD_DIM), lambda b, i: (0, 0)),
            pl.BlockSpec((1, 2 * HEAD_DIM), lambda b, i: (0, 0)),
        ],
        out_specs=pl.BlockSpec((None, tq, nq), lambda b, i: (b, i, 0)),
        out_shape=jax.ShapeDtypeStruct((n_batch, seq, nq), F32),
        scratch_shapes=[pltpu.VMEM((n_map, DIFF_GROUP * tq, HEAD_DIM), BF16)]
        + [pltpu.VMEM((n_map, DIFF_GROUP * tq, LANES), F32)] * 3,
        compiler_params=_cp("parallel", "arbitrary"),
        name="diff_attn",
    )(p3, p3, lam_vec.astype(F32), subln_g.reshape(1, -1).astype(F32))
    return o.reshape(n_batch * seq, nq)


def _diff_sample_kernel(pt_ref, q_ref, kv_ref, lam_ref, sub_ref, *rest, pps):
    pages = rest[:pps]
    o_ref, qb_ref, m_ref, l_ref, acc_ref, nk_ref = rest[pps:]
    step = pl.program_id(1)
    n_step = pl.num_programs(1)
    s_len = q_ref.shape[0]
    dh = HEAD_DIM
    dv = 2 * dh
    kw = DIFF_KV_HEADS * dv
    hr = DIFF_GROUP * 2 * s_len
    stride = 2 * DIFF_KV_HEADS

    @pl.when(step == 0)
    def _():
        qb_ref[...] = jnp.zeros_like(qb_ref)
        for idx in range(DIFF_KV_HEADS * DIFF_GROUP * 2):
            mp = idx % 2
            qb_ref[idx * s_len:(idx + 1) * s_len, mp * dh:(mp + 1) * dh] = q_ref[:, idx * dh:(idx + 1) * dh]
        m_ref[...] = jnp.full_like(m_ref, NEG)
        l_ref[...] = jnp.zeros_like(l_ref)
        acc_ref[...] = jnp.zeros_like(acc_ref)

    def update(h, parts, v):
        rows = slice(h * hr, (h + 1) * hr)
        mx = parts[0]
        for t in parts[1:]:
            mx = jnp.maximum(mx, t)
        m_old = m_ref[rows]
        m_new = jnp.maximum(m_old, jnp.max(mx, axis=-1, keepdims=True))
        corr = jnp.exp(m_old - m_new)
        ps = [jnp.exp(t - m_new) for t in parts]
        l_new = l_ref[rows] * corr
        for t in ps:
            l_new = l_new + t
        l_ref[rows] = l_new
        p = ps[0] if len(ps) == 1 else jnp.concatenate(ps, axis=1)
        acc_ref[rows] = acc_ref[rows] * corr + _dot(p.astype(BF16), v)
        m_ref[rows] = m_new

    for h in range(DIFF_KV_HEADS):
        qh = qb_ref[h * hr:(h + 1) * hr, :].astype(BF16)
        parts = [_dot_nt(qh, pg[pl.ds(h, PAGE_SIZE, stride=stride), :].astype(BF16)) for pg in pages]
        v = jnp.concatenate([pg[pl.ds(DIFF_KV_HEADS + h, PAGE_SIZE, stride=stride), :] for pg in pages],
                            axis=0).astype(BF16)
        update(h, parts, v)

    @pl.when(step == n_step - 1)
    def _():
        nk_ref[...] = jnp.zeros_like(nk_ref)
        nk_ref[:s_len, :] = kv_ref[...]
        r_s = lax.broadcasted_iota(jnp.int32, (hr, PAGE_SIZE), 0) % s_len
        c_s = lax.broadcasted_iota(jnp.int32, (hr, PAGE_SIZE), 1)
        for h in range(DIFF_KV_HEADS):
            qh = qb_ref[h * hr:(h + 1) * hr, :].astype(BF16)
            s_new = jnp.where(c_s <= r_s, _dot_nt(qh, nk_ref[:, h * dv:(h + 1) * dv].astype(BF16)), NEG)
            update(h, [s_new], nk_ref[:, kw + h * dv:kw + (h + 1) * dv].astype(BF16))
        lam = _diff_lambda(lam_ref)
        res = acc_ref[...] / jnp.sum(l_ref[...], axis=-1, keepdims=True)
        for hq in range(DIFF_KV_HEADS * DIFF_GROUP):
            a1 = res[(hq * 2) * s_len:(hq * 2 + 1) * s_len]
            a2 = res[(hq * 2 + 1) * s_len:(hq * 2 + 2) * s_len]
            o_ref[:, hq * dv:(hq + 1) * dv] = _diff_head_out(a1, a2, lam, sub_ref[...])


def _diff_sample(p, n_batch, s_len, cache, page_table, lam_vec, subln_g):
    n_in = p.shape[1]
    nq = DIFF_KV_HEADS * DIFF_GROUP * 2 * HEAD_DIM
    n_pages = page_table.shape[1]
    pps = math.gcd(n_pages, 8)
    n_rows = DIFF_KV_HEADS * DIFF_GROUP * 2 * s_len
    p3 = p.reshape(n_batch, s_len, n_in)
    kvw = 2 * DIFF_KV_HEADS * 2 * HEAD_DIM
    assert 2 * HEAD_DIM == LANES and PAGE_SIZE == LANES
    page_rows = PAGE_SIZE * 2 * DIFF_KV_HEADS
    cache3 = cache.reshape(cache.shape[0] * page_rows, LANES)

    def page_spec(jj):
        return pl.BlockSpec((page_rows, LANES), lambda b, s, pt: (pt[b * n_pages + s * pps + jj], 0))

    grid_spec = pltpu.PrefetchScalarGridSpec(
        num_scalar_prefetch=1,
        grid=(n_batch, n_pages // pps),
        in_specs=[
            pl.BlockSpec((None, s_len, nq), lambda b, s, pt: (b, 0, 0)),
            pl.BlockSpec((None, s_len, nq), lambda b, s, pt: (b, 0, 1)),
            pl.BlockSpec((4, HEAD_DIM), lambda b, s, pt: (0, 0)),
            pl.BlockSpec((1, 2 * HEAD_DIM), lambda b, s, pt: (0, 0)),
        ] + [page_spec(jj) for jj in range(pps)],
        out_specs=pl.BlockSpec((None, s_len, nq), lambda b, s, pt: (b, 0, 0)),
        scratch_shapes=[pltpu.VMEM((n_rows, LANES), F32)] * 4 + [pltpu.VMEM((PAGE_SIZE, kvw), F32)],
    )
    o = pl.pallas_call(
        functools.partial(_diff_sample_kernel, pps=pps),
        grid_spec=grid_spec,
        out_shape=jax.ShapeDtypeStruct((n_batch, s_len, nq), F32),
        compiler_params=_cp("parallel", "arbitrary"),
        name="diff_decode",
    )(page_table.reshape(-1), p3, p3, lam_vec.astype(F32), subln_g.reshape(1, -1).astype(F32),
      *([cache3] * pps))
    return o.reshape(n_batch * s_len, nq)


def kernel(x_prompt, x_sample, state_gla, cache_swa, cache_dil1, cache_dil2, cache_dil3, cache_diff, page_table, c_prompt, c_sample, norm1_g, norm2_g, ada_w, ada_b, gla_w_in, gla_w_gate2, gla_b_gate, gla_onorm_g, gla_w_out, swa_w_in, swa_q_norm, swa_k_norm, swa_sinks, swa_w_out, dil_w_in, dil_q_norm, dil_k_norm, dil_w_out, diff_w_in, diff_q_norm, diff_k_norm, diff_lambda, diff_subln_g, diff_w_out, ffn_w_gu, ffn_w_down, moe_router, moe_w_gu, moe_w_down):
    bp, seq, d = x_prompt.shape
    db, s_len, _ = x_sample.shape
    depth = ada_w.shape[0]
    dh = HEAD_DIM
    qscale = dh ** -0.5

    mods = _ada(jnp.concatenate([c_prompt, c_sample], axis=0), ada_w, ada_b)
    xp = x_prompt.reshape(bp * seq, d)
    xs = x_sample.reshape(db * s_len, d)
    out = {}

    for i in range(depth):
        gp = _Group(bp, seq, mods[i, :bp])
        gs = _Group(db, s_len, mods[i, bp:])
        kind = i % 4
        if kind == 0:
            n_in = gla_w_in.shape[1]
            n_pad = -(-n_in // (5 * LANES)) * (5 * LANES)
            w_in = jnp.pad(gla_w_in, ((0, 0), (0, n_pad - n_in)))
            lr_w = n_pad - (n_in - GLA_RANK)
            wg2 = jnp.pad(gla_w_gate2, ((0, lr_w - GLA_RANK), (0, 0)))
            pp = _inproj(gp, xp, norm1_g[i], 0, w_in, n_pad // 5)
            ps = _inproj(gs, xs, norm1_g[i], 0, w_in, n_pad // 5)
            op, out["gla_p"] = _gla(pp, bp, seq, wg2, gla_b_gate, gla_onorm_g, None)
            os_, out["gla_s"] = _gla(ps, db, s_len, wg2, gla_b_gate, gla_onorm_g, state_gla)
            xp = _outproj(gp, xp, op, gla_w_out, 2)
            xs = _outproj(gs, xs, os_, gla_w_out, 2)
        elif kind == 1:
            hk = SWA_KV_HEADS
            nq = swa_w_out.shape[0]
            grp = nq // dh // hk
            kvw = hk * dh
            norm = _norm_rows([(nq, swa_q_norm, qscale), (kvw, swa_k_norm, 1.0), (kvw, None, 1.0)])
            pp = _inproj(gp, xp, norm1_g[i], 0, swa_w_in, kvw * 2, norm)
            ps = _inproj(gs, xs, norm1_g[i], 0, swa_w_in, kvw * 2, norm)
            (op,) = _banded(pp, bp, seq, 1, (nq, 0), (kvw, nq // kvw), (kvw, nq // kvw + 1),
                            hk, grp, SWA_WINDOW, sinks=swa_sinks)
            keep = min(SWA_WINDOW, seq)
            out["swa_p"] = pp.reshape(bp, seq, -1)[:, seq - keep:, nq:].reshape(bp, keep, 2, hk, dh)
            os_, out["swa_s"] = _win_decode(ps, db, s_len, cache_swa, (nq, 0), (kvw, nq // kvw),
                                            (kvw, nq // kvw + 1), hk, grp, 1, SWA_WINDOW, sinks=swa_sinks)
            xp = _outproj(gp, xp, op, swa_w_out, 2)
            xs = _outproj(gs, xs, os_, swa_w_out, 2)
        elif kind == 2:
            w = DIL_HEADS * dh
            norm = _norm_rows([(w, dil_q_norm, qscale), (w, dil_k_norm, 1.0), (w, None, 1.0)] * len(DIL_GROUPS))
            pp = _inproj(gp, xp, norm1_g[i], 0, dil_w_in, w, norm)
            ps = _inproj(gs, xs, norm1_g[i], 0, dil_w_in, w, norm)
            outs_p, lses_p, outs_s, lses_s = [], [], [], []
            caches = (cache_dil1, cache_dil2, cache_dil3)
            for gi, (win, dil) in enumerate(DIL_GROUPS):
                o, lse = _banded(pp, bp, seq, dil, (w, 3 * gi), (w, 3 * gi + 1), (w, 3 * gi + 2),
                                 DIL_HEADS, 1, win // dil, want_lse=True)
                outs_p.append(o)
                lses_p.append(lse)
                keep = min(win, seq)
                c0 = (3 * gi + 1) * w
                out["dil%d_p" % gi] = pp.reshape(bp, seq, -1)[:, seq - keep:, c0:c0 + 2 * w].reshape(
                    bp, keep, 2, DIL_HEADS, dh)
                o, lse, out["dil%d_s" % gi] = _win_decode(ps, db, s_len, caches[gi], (w, 3 * gi), (w, 3 * gi + 1),
                                                          (w, 3 * gi + 2), DIL_HEADS, 1, dil, win, want_lse=True)
                outs_s.append(o)
                lses_s.append(lse)
            xp = _outproj_mix(gp, xp, outs_p, lses_p, dil_w_out, 2)
            xs = _outproj_mix(gs, xs, outs_s, lses_s, dil_w_out, 2)
        else:
            nq = DIFF_KV_HEADS * DIFF_GROUP * 2 * dh
            nk = DIFF_KV_HEADS * 2 * dh
            norm = _norm_rows([(nq, diff_q_norm, qscale), (nk, diff_k_norm, 1.0), (nk, None, 1.0)])
            pp = _inproj(gp, xp, norm1_g[i], 0, diff_w_in, nk, norm)
            ps = _inproj(gs, xs, norm1_g[i], 0, diff_w_in, nk, norm)
            op = _diff_prompt(pp, bp, seq, diff_lambda, diff_subln_g)
            os_ = _diff_sample(ps, db, s_len, cache_diff, page_table, diff_lambda, diff_subln_g)
            out["diff_p"] = pp[:, nq:].reshape(bp, seq, 2, DIFF_KV_HEADS, 2 * dh)
            out["diff_s"] = ps[:, nq:].reshape(db, s_len, 2, DIFF_KV_HEADS, 2 * dh)
            xp = _outproj(gp, xp, op, diff_w_out, 2)
            xs = _outproj(gs, xs, os_, diff_w_out, 2)
        j = i // 2
        if i % 2 == 0:
            xp = _ffn(gp, xp, norm2_g[i], ffn_w_gu[j], ffn_w_down[j], 256)
            xs = _ffn(gs, xs, norm2_g[i], ffn_w_gu[j], ffn_w_down[j], 256)
        else:
            xp = _moe(gp, xp, norm2_g[i], moe_router[j], moe_w_gu[j], moe_w_down[j], 512)
            xs = _moe(gs, xs, norm2_g[i], moe_router[j], moe_w_gu[j], moe_w_down[j], 512)

    return (xp.reshape(bp, seq, d), xs.reshape(db, s_len, d), out["gla_p"], out["gla_s"],
            out["swa_p"], out["swa_s"], out["dil0_p"], out["dil0_s"], out["dil1_p"], out["dil1_s"],
            out["dil2_p"], out["dil2_s"], out["diff_p"], out["diff_s"])
```

```python
import functools
import math

import jax
import jax.numpy as jnp
from jax import lax
from jax.experimental import pallas as pl
from jax.experimental.pallas import tpu as pltpu

F32 = jnp.float32
BF16 = jnp.bfloat16

EPS = 1e-6
NEG = -1e30
BLK = 128
HEAD_DIM = 64

GLA_HEADS = 4
GLA_RANK = 16
GLA_NORMALIZER = 16.0
GLA_CHUNK = 64

SWA_KV_HEADS = 4
SWA_WINDOW = 128
DIL_GROUPS = ((128, 1), (512, 4), (2048, 16))
DIL_HEADS = 8
DIFF_KV_HEADS = 4
DIFF_GROUP = 2
LAMBDA_INIT = 0.8 - 0.6 * math.exp(-0.3 * 3)
N_EXPERTS = 8
PAGE_SIZE = 128

LANES = 128
VMEM_LIMIT = 56 * 1024 * 1024


def _cp(*sem):
    return pltpu.CompilerParams(dimension_semantics=sem, vmem_limit_bytes=VMEM_LIMIT)


def _dot(a, b):
    return jnp.dot(a, b, preferred_element_type=F32)


def _dot_nt(a, b):
    return lax.dot_general(a, b, (((1,), (1,)), ((), ())), preferred_element_type=F32)


def _dot_tn(a, b):
    return lax.dot_general(a, b, (((0,), (0,)), ((), ())), preferred_element_type=F32)


def _silu(x):
    return x / (1.0 + jnp.exp(-x))


def _adaln(x, g, shift, scale):
    y = x * lax.rsqrt(jnp.mean(x * x, axis=-1, keepdims=True) + EPS)
    return y * g * (1.0 + scale) + shift


class _Group:
    def __init__(self, n_batch, rows_per_batch, mods):
        self.m = n_batch * rows_per_batch
        self.rows_per_batch = rows_per_batch
        self.prompt = rows_per_batch >= 256
        d6 = mods.shape[-1]
        if self.prompt:
            self.mods = mods.reshape(n_batch, 1, d6)
        else:
            self.mods = jnp.repeat(mods, rows_per_batch, axis=0)
        self.d = d6 // 6
        self.tm = self.tile(1024)

    def tile(self, tm_max):
        return math.gcd(self.rows_per_batch if self.prompt else self.m, tm_max)

    def mod_spec(self, k, tm=None):
        d = self.d
        tm = tm or self.tm
        if self.prompt:
            tpb = self.rows_per_batch // tm
            return pl.BlockSpec((None, 1, d), lambda i, *_: (i // tpb, 0, k))
        return pl.BlockSpec((tm, d), lambda i, *_: (i, k))


def _ada_kernel(c_ref, w_ref, b_ref, o_ref):
    a = _silu(c_ref[...]).astype(BF16)
    o_ref[...] = _dot(a, w_ref[...].astype(BF16)) + b_ref[...]


def _ada(c_all, ada_w, ada_b):
    depth, d, d6 = ada_w.shape
    nb = c_all.shape[0]
    tn = 1024
    return pl.pallas_call(
        _ada_kernel,
        grid=(depth, d6 // tn),
        in_specs=[
            pl.BlockSpec((nb, d), lambda l, j: (0, 0)),
            pl.BlockSpec((None, d, tn), lambda l, j: (l, 0, j)),
            pl.BlockSpec((None, 1, tn), lambda l, j: (l, 0, j)),
        ],
        out_specs=pl.BlockSpec((None, nb, tn), lambda l, j: (l, 0, j)),
        out_shape=jax.ShapeDtypeStruct((depth, nb, d6), F32),
        compiler_params=_cp("parallel", "parallel"),
        name="ada_mod",
    )(c_all, ada_w, ada_b.reshape(depth, 1, d6))


def _inproj_kernel(x_ref, g_ref, sh_ref, sc_ref, w_ref, *rest, norm):
    if norm:
        flag_ref, gain_ref, o_ref, h_ref = rest
    else:
        o_ref, h_ref = rest
    j = pl.program_id(1)

    @pl.when(j == 0)
    def _():
        h_ref[...] = _adaln(x_ref[...], g_ref[...], sh_ref[...], sc_ref[...]).astype(BF16)

    acc = _dot(h_ref[...], w_ref[...].astype(BF16))
    if not norm:
        o_ref[...] = acc
        return
    tn = acc.shape[1]
    lo = lax.broadcasted_iota(jnp.int32, (1, LANES), 1) < HEAD_DIM
    for c in range(tn // LANES):
        sl = slice(c * LANES, (c + 1) * LANES)
        a = acc[:, sl]
        sq = a * a
        s_lo = jnp.sum(jnp.where(lo, sq, 0.0), axis=-1, keepdims=True)
        s_hi = jnp.sum(jnp.where(lo, 0.0, sq), axis=-1, keepdims=True)
        r = lax.rsqrt(jnp.where(lo, s_lo, s_hi) * (1.0 / HEAD_DIM) + EPS)
        o_ref[:, sl] = a * jnp.where(flag_ref[:, sl] > 0.0, r, 1.0) * gain_ref[:, sl]


def _inproj(grp, x, norm_g, k_shift, w, tn, norm=None):
    m, d = x.shape
    n = w.shape[1]
    tm = grp.tm
    in_specs = [
        pl.BlockSpec((tm, d), lambda i, j: (i, 0)),
        pl.BlockSpec((1, d), lambda i, j: (0, 0)),
        grp.mod_spec(k_shift),
        grp.mod_spec(k_shift + 1),
        pl.BlockSpec((d, tn), lambda i, j: (0, j)),
    ]
    args = [x, norm_g.reshape(1, d), grp.mods, grp.mods, w]
    if norm is not None:
        in_specs += [pl.BlockSpec((1, tn), lambda i, j: (0, j))] * 2
        args += list(norm)
    return pl.pallas_call(
        functools.partial(_inproj_kernel, norm=norm is not None),
        grid=(m // tm, n // tn),
        in_specs=in_specs,
        out_specs=pl.BlockSpec((tm, tn), lambda i, j: (i, j)),
        out_shape=jax.ShapeDtypeStruct((m, n), F32),
        scratch_shapes=[pltpu.VMEM((tm, d), BF16)],
        compiler_params=_cp("parallel", "arbitrary"),
        name="adaln_inproj",
    )(*args)


def _norm_rows(pieces):
    flags, gains = [], []
    for width, g, scale in pieces:
        if g is None:
            flags.append(jnp.zeros((width,), F32))
            gains.append(jnp.ones((width,), F32))
        else:
            flags.append(jnp.ones((width,), F32))
            gains.append(jnp.tile(g.astype(F32) * scale, width // HEAD_DIM))
    return jnp.concatenate(flags)[None, :], jnp.concatenate(gains)[None, :]


def _outproj_kernel(x_ref, o_ref, gt_ref, w_ref, out_ref):
    y = _dot(o_ref[...].astype(BF16), w_ref[...].astype(BF16))
    out_ref[...] = x_ref[...] + gt_ref[...] * y


def _outproj(grp, x, o, w, k_gate):
    m, d = x.shape
    kdim = w.shape[0]
    tm = grp.tm
    return pl.pallas_call(
        _outproj_kernel,
        grid=(m // tm,),
        in_specs=[
            pl.BlockSpec((tm, d), lambda i: (i, 0)),
            pl.BlockSpec((tm, kdim), lambda i: (i, 0)),
            grp.mod_spec(k_gate),
            pl.BlockSpec((kdim, d), lambda i: (0, 0)),
        ],
        out_specs=pl.BlockSpec((tm, d), lambda i: (i, 0)),
        out_shape=jax.ShapeDtypeStruct((m, d), F32),
        compiler_params=_cp("parallel"),
        name="outproj",
    )(x, o, grp.mods, w)


def _outproj_mix_kernel(x_ref, o0_ref, o1_ref, o2_ref, l0_ref, l1_ref, l2_ref, gt_ref, w_ref, out_ref):
    l0, l1, l2 = l0_ref[...], l1_ref[...], l2_ref[...]
    mx = jnp.maximum(jnp.maximum(l0, l1), l2)
    e0, e1, e2 = jnp.exp(l0 - mx), jnp.exp(l1 - mx), jnp.exp(l2 - mx)
    o = (e0 * o0_ref[...] + e1 * o1_ref[...] + e2 * o2_ref[...]) / (e0 + e1 + e2)
    y = _dot(o.astype(BF16), w_ref[...].astype(BF16))
    out_ref[...] = x_ref[...] + gt_ref[...] * y


def _outproj_mix(grp, x, outs, lses, w, k_gate):
    m, d = x.shape
    kdim = w.shape[0]
    tm = grp.tm
    row = pl.BlockSpec((tm, kdim), lambda i: (i, 0))
    return pl.pallas_call(
        _outproj_mix_kernel,
        grid=(m // tm,),
        in_specs=[pl.BlockSpec((tm, d), lambda i: (i, 0))] + [row] * 6
        + [grp.mod_spec(k_gate), pl.BlockSpec((kdim, d), lambda i: (0, 0))],
        out_specs=pl.BlockSpec((tm, d), lambda i: (i, 0)),
        out_shape=jax.ShapeDtypeStruct((m, d), F32),
        compiler_params=_cp("parallel"),
        name="outproj_mix",
    )(x, *outs, *lses, grp.mods, w)


def _ffn_kernel(x_ref, g_ref, sh_ref, sc_ref, gt_ref, wg_ref, wu_ref, wd_ref, o_ref, h_ref):
    f = pl.program_id(1)
    nf = pl.num_programs(1)

    @pl.when(f == 0)
    def _():
        h_ref[...] = _adaln(x_ref[...], g_ref[...], sh_ref[...], sc_ref[...]).astype(BF16)
        o_ref[...] = jnp.zeros_like(o_ref)

    h = h_ref[...]
    gate = _dot(h, wg_ref[...])
    up = _dot(h, wu_ref[...])
    a = (_silu(gate) * up).astype(BF16)
    o_ref[...] += _dot(a, wd_ref[...])

    @pl.when(f == nf - 1)
    def _():
        o_ref[...] = x_ref[...] + gt_ref[...] * o_ref[...]


def _ffn(grp, x, norm_g, w_gu, w_down, tf):
    m, d = x.shape
    dff = w_down.shape[0]
    tm = grp.tm
    nf = dff // tf
    return pl.pallas_call(
        _ffn_kernel,
        grid=(m // tm, nf),
        in_specs=[
            pl.BlockSpec((tm, d), lambda i, f: (i, 0)),
            pl.BlockSpec((1, d), lambda i, f: (0, 0)),
            grp.mod_spec(3), grp.mod_spec(4), grp.mod_spec(5),
            pl.BlockSpec((d, tf), lambda i, f: (0, f)),
            pl.BlockSpec((d, tf), lambda i, f: (0, f + nf)),
            pl.BlockSpec((tf, d), lambda i, f: (f, 0)),
        ],
        out_specs=pl.BlockSpec((tm, d), lambda i, f: (i, 0)),
        out_shape=jax.ShapeDtypeStruct((m, d), F32),
        scratch_shapes=[pltpu.VMEM((tm, d), BF16)],
        compiler_params=_cp("parallel", "arbitrary"),
        name="ffn_swiglu",
    )(x, norm_g.reshape(1, d), grp.mods, grp.mods, grp.mods, w_gu, w_gu, w_down)


def _route_top2(logits):
    lane = lax.broadcasted_iota(jnp.int32, logits.shape, 1)
    lg = jnp.where(lane < N_EXPERTS, logits, -jnp.inf)
    m1 = jnp.max(lg, axis=-1, keepdims=True)
    i1 = jnp.min(jnp.where(lg == m1, lane, LANES), axis=-1, keepdims=True)
    lg2 = jnp.where(lane == i1, -jnp.inf, lg)
    m2 = jnp.max(lg2, axis=-1, keepdims=True)
    i2 = jnp.min(jnp.where(lg2 == m2, lane, LANES), axis=-1, keepdims=True)
    e2 = jnp.exp(m2 - m1)
    g1 = 1.0 / (1.0 + e2)
    g2 = e2 / (1.0 + e2)
    comb = jnp.where(lane == i1, g1, 0.0) + jnp.where(lane == i2, g2, 0.0)
    return comb, (lane == i1) | (lane == i2)


def _moe_kernel(x_ref, g_ref, sh_ref, sc_ref, gt_ref, r_ref, wg_ref, wu_ref, wd_ref, o_ref,
                h_ref, comb_ref, key_ref, keyt_ref, cnt_ref, xs_ref, acc_ref, n_ref, *, sub):
    e = pl.program_id(1)
    f = pl.program_id(2)
    nf = pl.num_programs(2)
    tm = h_ref.shape[0]
    lane = lax.broadcasted_iota(jnp.int32, (tm, LANES), 1)

    @pl.when((e == 0) & (f == 0))
    def _():
        x = x_ref[...]
        h = _adaln(x, g_ref[...], sh_ref[...], sc_ref[...]).astype(BF16)
        h_ref[...] = h
        comb, sel = _route_top2(_dot(h, r_ref[...].astype(BF16)))
        comb_ref[...] = comb
        tb = math.gcd(tm, 256)
        tri = jnp.where(lax.broadcasted_iota(jnp.int32, (tb, tb), 0) >= lax.broadcasted_iota(jnp.int32, (tb, tb), 1),
                        1.0, 0.0).astype(BF16)
        carry = jnp.zeros((1, LANES), F32)
        for blk in range(tm // tb):
            rows = slice(blk * tb, (blk + 1) * tb)
            sb = jnp.where(sel[rows], 1.0, 0.0)
            incl = _dot(tri, sb.astype(BF16)) + carry
            key_ref[rows, :] = jnp.where(sb > 0.0, incl - 1.0, -1.0).astype(jnp.int32)
            carry = carry + jnp.sum(sb, axis=0, keepdims=True)
        cnt_ref[...] = carry
        keyt_ref[...] = key_ref[...].T
        o_ref[...] = x

    def onehot(r0, nr):
        keyrow = keyt_ref[pl.ds(e, 1), :]
        r = lax.broadcasted_iota(jnp.int32, (nr, tm), 0) + r0
        return jnp.where(keyrow == r, 1.0, 0.0).astype(BF16)

    def for_blocks(fn):
        for r0 in range(0, tm, 2 * sub):
            if r0 + sub < tm:
                pl.when(r0 + sub < n_e)(functools.partial(fn, r0, 2 * sub))
                pl.when((r0 < n_e) & (n_e <= r0 + sub))(functools.partial(fn, r0, sub))
            else:
                pl.when(r0 < n_e)(functools.partial(fn, r0, sub))

    @pl.when(f == 0)
    def _():
        cnt = jnp.sum(jnp.where(lane[:1] == e, cnt_ref[...], 0.0), axis=-1, keepdims=True)
        n_ref[0] = cnt.astype(jnp.int32)[0, 0]

    n_e = n_ref[0]

    @pl.when(f == 0)
    def _():
        def pack(r0, nr):
            xs_ref[r0:r0 + nr, :] = _dot(onehot(r0, nr), h_ref[...]).astype(BF16)
            acc_ref[r0:r0 + nr, :] = jnp.zeros((nr, acc_ref.shape[1]), F32)
        for_blocks(pack)

    def expert(r0, nr):
        xb = xs_ref[r0:r0 + nr, :]
        a = (_silu(_dot(xb, wg_ref[...])) * _dot(xb, wu_ref[...])).astype(BF16)
        acc_ref[r0:r0 + nr, :] += _dot(a, wd_ref[...])
    for_blocks(expert)

    @pl.when(f == nf - 1)
    def _():
        ce = jnp.sum(jnp.where(lane == e, comb_ref[...], 0.0), axis=-1, keepdims=True)
        c_hi = ce.astype(BF16).astype(F32)
        c_mid = (ce - c_hi).astype(BF16).astype(F32)
        c_lo = ce - c_hi - c_mid
        cw = jnp.where(lane == 0, c_hi, jnp.where(lane == 1, c_mid, jnp.where(lane == 2, c_lo, 0.0))).astype(BF16)
        kcol = jnp.sum(jnp.where(lane == e, key_ref[...].astype(F32), 0.0), axis=-1,
                       keepdims=True).astype(jnp.int32)

        def unpack(r0, nr):
            g3 = _dot(onehot(r0, nr), cw)
            gate = g3[:, 0:1] + g3[:, 1:2] + g3[:, 2:3]
            y = (acc_ref[r0:r0 + nr, :] * gate).astype(BF16)
            c = lax.broadcasted_iota(jnp.int32, (tm, nr), 1) + r0
            scatter = jnp.where(kcol == c, 1.0, 0.0).astype(BF16)
            o_ref[...] += gt_ref[...] * _dot(scatter, y)
        for_blocks(unpack)


def _moe(grp, x, norm_g, router, w_gu, w_down, tf):
    m, d = x.shape
    ne, dff, _ = w_down.shape
    tm = grp.tile(2048)
    sub = math.gcd(tm, 256)
    nf = dff // tf
    router_p = jnp.pad(router, ((0, 0), (0, LANES - ne)))
    resident = dict(pipeline_mode=pl.Buffered(1))
    return pl.pallas_call(
        functools.partial(_moe_kernel, sub=sub),
        grid=(m // tm, ne, nf),
        in_specs=[
            pl.BlockSpec((tm, d), lambda i, e, f: (i, 0), **resident),
            pl.BlockSpec((1, d), lambda i, e, f: (0, 0)),
            grp.mod_spec(3, tm), grp.mod_spec(4, tm), grp.mod_spec(5, tm),
            pl.BlockSpec((d, LANES), lambda i, e, f: (0, 0)),
            pl.BlockSpec((None, d, tf), lambda i, e, f: (e, 0, f)),
            pl.BlockSpec((None, d, tf), lambda i, e, f: (e, 0, f + nf)),
            pl.BlockSpec((None, tf, d), lambda i, e, f: (e, f, 0)),
        ],
        out_specs=pl.BlockSpec((tm, d), lambda i, e, f: (i, 0), **resident),
        out_shape=jax.ShapeDtypeStruct((m, d), F32),
        scratch_shapes=[
            pltpu.VMEM((tm, d), BF16), pltpu.VMEM((tm, LANES), F32), pltpu.VMEM((tm, LANES), jnp.int32),
            pltpu.VMEM((LANES, tm), jnp.int32), pltpu.VMEM((1, LANES), F32),
            pltpu.VMEM((tm, d), BF16), pltpu.VMEM((tm, d), F32),
            pltpu.SMEM((1,), jnp.int32),
        ],
        compiler_params=_cp("parallel", "arbitrary", "arbitrary"),
        name="moe",
    )(x, norm_g.reshape(1, d), grp.mods, grp.mods, grp.mods, router_p, w_gu, w_gu, w_down)


def _gla_kernel(p_ref, wg2_ref, bg_ref, on_ref, *rest, chunk, n_chunk, n_valid, has_s0):
    rest = list(rest)
    s0_ref = rest.pop(0) if has_s0 else None
    o_ref, sfin_ref, s_ref = rest[:3]
    if n_valid < chunk:
        src = rest[3]
        src[...] = jnp.zeros_like(src)
        src[:n_valid, :] = p_ref[...]
    else:
        src = p_ref
    li = pl.program_id(1)
    nl = pl.num_programs(1)
    nq = GLA_HEADS * 128
    nv = GLA_HEADS * 256
    dk, dv = 128, 256

    @pl.when(li == 0)
    def _():
        if has_s0:
            s_ref[...] = s0_ref[...]
        else:
            s_ref[...] = jnp.zeros_like(s_ref)

    c = chunk
    row = lax.broadcasted_iota(jnp.int32, (c, c), 0)
    col = lax.broadcasted_iota(jnp.int32, (c, c), 1)
    causal = row >= col
    tri = jnp.where(causal, 1.0, 0.0).astype(BF16)
    ones = jnp.ones((c, dv), BF16)
    live = None
    if n_valid < c:
        live = lax.broadcasted_iota(jnp.int32, (c, 1), 0) < n_valid
    for ci in range(n_chunk):
        rows = slice(ci * c, (ci + 1) * c)
        a_lr = src[rows, 2 * nq + 2 * nv:]
        z = _dot(a_lr.astype(BF16), wg2_ref[...].astype(BF16)) + bg_ref[...]
        la = (jnp.minimum(z, 0.0) - jnp.log1p(jnp.exp(-jnp.abs(z)))) * (1.0 / GLA_NORMALIZER)
        if live is not None:
            la = jnp.where(live, la, 0.0)
        hi = la.astype(BF16)
        lo = (la - hi.astype(F32)).astype(BF16)
        b = _dot(tri, hi) + _dot(tri, lo)
        for h in range(GLA_HEADS):
            ks = slice(h * dk, (h + 1) * dk)
            vs = slice(h * dv, (h + 1) * dv)
            bh = b[:, ks]
            bl = bh[c - 1:c, :]
            qh = src[rows, h * dk:(h + 1) * dk] * (dk ** -0.5)
            kh = src[rows, nq + h * dk:nq + (h + 1) * dk]
            vh = src[rows, 2 * nq + h * dv:2 * nq + (h + 1) * dv]
            gh = src[rows, 2 * nq + nv + h * dv:2 * nq + nv + (h + 1) * dv]
            vb = vh.astype(BF16)
            qd = (qh * jnp.exp(bh)).astype(BF16)
            ki = (kh * jnp.exp(-bh)).astype(BF16)
            kd = (kh * jnp.exp(bl - bh)).astype(BF16)
            att = jnp.where(causal, _dot_nt(qd, ki), 0.0).astype(BF16)
            s = s_ref[h]
            o = _dot(att, vb) + _dot(qd, s.astype(BF16))
            dl = _dot_tn(hi[:, ks], ones) + _dot_tn(lo[:, ks], ones)
            s_ref[h] = jnp.exp(dl) * s + _dot_tn(kd, vb)
            on = o * lax.rsqrt(jnp.mean(o * o, axis=-1, keepdims=True) + EPS) * on_ref[...]
            res = on * _silu(gh)
            if n_valid < c:
                o_ref[:, vs] = res[:n_valid]
            else:
                o_ref[rows, vs] = res

    @pl.when(li == nl - 1)
    def _():
        sfin_ref[...] = s_ref[...]


def _gla(p, n_batch, seq, w_gate2_p, b_gate, onorm_g, s0):
    n_in = p.shape[1]
    nv = GLA_HEADS * 256
    c_ref = math.gcd(seq, GLA_CHUNK)
    if seq >= GLA_CHUNK:
        chunk, rows, n_valid = c_ref, math.gcd(seq, 256), c_ref
        scratch = []
    else:
        chunk, rows, n_valid = GLA_CHUNK, seq, seq
        scratch = [pltpu.VMEM((chunk, n_in), F32)]
    n_l = seq // rows
    p3 = p.reshape(n_batch, seq, n_in)
    has_s0 = s0 is not None
    st_spec = pl.BlockSpec((None, GLA_HEADS, 128, 256), lambda b, l: (b, 0, 0, 0))
    in_specs = [
        pl.BlockSpec((None, rows, n_in), lambda b, l: (b, l, 0)),
        pl.BlockSpec((LANES, GLA_HEADS * 128), lambda b, l: (0, 0)),
        pl.BlockSpec((1, GLA_HEADS * 128), lambda b, l: (0, 0)),
        pl.BlockSpec((1, 256), lambda b, l: (0, 0)),
    ]
    args = [p3, w_gate2_p, b_gate.reshape(1, -1), onorm_g.reshape(1, -1)]
    if has_s0:
        in_specs.append(st_spec)
        args.append(s0)
    o, s_fin = pl.pallas_call(
        functools.partial(_gla_kernel, chunk=chunk, n_chunk=max(rows // chunk, 1), n_valid=n_valid,
                          has_s0=has_s0),
        grid=(n_batch, n_l),
        in_specs=in_specs,
        out_specs=[pl.BlockSpec((None, rows, nv), lambda b, l: (b, l, 0)), st_spec],
        out_shape=[jax.ShapeDtypeStruct((n_batch, seq, nv), F32),
                   jax.ShapeDtypeStruct((n_batch, GLA_HEADS, 128, 256), F32)],
        scratch_shapes=[pltpu.VMEM((GLA_HEADS, 128, 256), F32)] + scratch,
        compiler_params=_cp("parallel", "arbitrary"),
        name="gla",
    )(*args)
    return o.reshape(n_batch * seq, nv), s_fin


def _banded_kernel(q_ref, kc_ref, kp_ref, vc_ref, vp_ref, *rest, hk, grp, dil, window, has_sink, want_lse):
    rest = list(rest)
    sink_ref = rest.pop(0) if has_sink else None
    o_ref = rest.pop(0)
    lse_ref = rest.pop(0) if want_lse else None
    n = pl.program_id(1)
    i = lax.broadcasted_iota(jnp.int32, (BLK, BLK), 0)
    j = lax.broadcasted_iota(jnp.int32, (BLK, BLK), 1)
    valid_c = (j <= i) & (i - j <= window)
    valid_p = i - j + BLK + jnp.where(n > 0, 0, 2 * BLK) <= window
    dh = HEAD_DIM
    for r in range(dil):
        rows = pl.ds(r, BLK, stride=dil) if dil > 1 else slice(None)
        q_all = q_ref[rows, :]
        kc_all, kp_all, vc_all, vp_all = kc_ref[rows, :], kp_ref[rows, :], vc_ref[rows, :], vp_ref[rows, :]
        outs, lses = [], []
        for h in range(hk):
            hs = slice(h * dh, (h + 1) * dh)
            kc = kc_all[:, hs].astype(BF16)
            kp = kp_all[:, hs].astype(BF16)
            vc = vc_all[:, hs].astype(BF16)
            vp = vp_all[:, hs].astype(BF16)
            for g in range(grp):
                qi = h * grp + g
                q = q_all[:, qi * dh:(qi + 1) * dh].astype(BF16)
                sc = jnp.where(valid_c, _dot_nt(q, kc), NEG)
                sp = jnp.where(valid_p, _dot_nt(q, kp), NEG)
                m = jnp.max(jnp.maximum(sc, sp), axis=-1, keepdims=True)
                if has_sink:
                    sk = sink_ref[:, qi:qi + 1]
                    m = jnp.maximum(m, sk)
                pc = jnp.exp(sc - m)
                pp = jnp.exp(sp - m)
                l = jnp.sum(pc + pp, axis=-1, keepdims=True)
                if has_sink:
                    l = l + jnp.exp(sk - m)
                outs.append((_dot(pc.astype(BF16), vc) + _dot(pp.astype(BF16), vp)) / l)
                if want_lse:
                    lses.append(jnp.broadcast_to(m + jnp.log(l), (BLK, dh)))
        o_ref[rows, :] = jnp.concatenate(outs, axis=1)
        if want_lse:
            lse_ref[rows, :] = jnp.concatenate(lses, axis=1)


def _banded(p, n_batch, seq, dil, q_blk, k_blk, v_blk, hk, grp, window, sinks=None, want_lse=False):
    n_in = p.shape[1]
    rows = dil * BLK
    assert seq % rows == 0
    nb = seq // rows
    p3 = p.reshape(n_batch, seq, n_in)
    qw, qi = q_blk
    kw, ki = k_blk
    vw, vi = v_blk
    if dil > 1:
        assert grp == 1 and qw == kw == vw and qw % LANES == 0
        n_cb = qw // LANES
        qi, ki, vi = qi * n_cb, ki * n_cb, vi * n_cb
        qw = kw = vw = LANES
        hk = LANES // HEAD_DIM
    else:
        n_cb = 1

    def cur(w, ci):
        return pl.BlockSpec((None, rows, w), lambda b, n, c: (b, n, ci + c))

    def prev(w, ci):
        return pl.BlockSpec((None, rows, w), lambda b, n, c: (b, jnp.maximum(n - 1, 0), ci + c))

    in_specs = [cur(qw, qi), cur(kw, ki), prev(kw, ki), cur(vw, vi), prev(vw, vi)]
    args = [p3] * 5
    if sinks is not None:
        in_specs.append(pl.BlockSpec((1, sinks.shape[0]), lambda b, n, c: (0, 0)))
        args.append(sinks.reshape(1, -1).astype(F32))
    o_spec = pl.BlockSpec((None, rows, qw), lambda b, n, c: (b, n, c))
    o_shape = jax.ShapeDtypeStruct((n_batch, seq, n_cb * qw), F32)
    outs = pl.pallas_call(
        functools.partial(_banded_kernel, hk=hk, grp=grp, dil=dil, window=window,
                          has_sink=sinks is not None, want_lse=want_lse),
        grid=(n_batch, nb, n_cb),
        in_specs=in_specs,
        out_specs=[o_spec, o_spec] if want_lse else [o_spec],
        out_shape=[o_shape, o_shape] if want_lse else [o_shape],
        compiler_params=_cp("parallel", "arbitrary", "arbitrary"),
        name="banded_attn",
    )(*args)
    return [t.reshape(n_batch * seq, n_cb * qw) for t in outs]


def _kv_tail_kernel(k_ref, v_ref, o_ref):
    w = k_ref.shape[1]
    o_ref[:w, :] = k_ref[...].T
    o_ref[w:, :] = v_ref[...].T


def _kv_tail(p, n_batch, seq, c0, hk, keep):
    n_in = p.shape[1]
    w = hk * HEAD_DIM
    rows = math.gcd(keep, 512)
    assert c0 % w == 0 and rows % LANES == 0 and (seq - keep) % rows == 0
    r0 = (seq - keep) // rows
    p3 = p.reshape(n_batch, seq, n_in)
    t = pl.pallas_call(
        _kv_tail_kernel,
        grid=(n_batch, keep // rows),
        in_specs=[pl.BlockSpec((None, rows, w), lambda b, i: (b, r0 + i, c0 // w)),
                  pl.BlockSpec((None, rows, w), lambda b, i: (b, r0 + i, c0 // w + 1))],
        out_specs=pl.BlockSpec((None, 2 * w, rows), lambda b, i: (b, 0, i)),
        out_shape=jax.ShapeDtypeStruct((n_batch, 2 * w, keep), F32),
        compiler_params=_cp("parallel", "parallel"),
        name="kv_tail",
    )(p3, p3)
    return jnp.transpose(t.reshape(n_batch, 2, hk, HEAD_DIM, keep), (0, 4, 1, 2, 3))


def _win_decode_kernel(q_ref, kn_ref, vn_ref, c_ref, *rest, bb, hk, grp, dil, win, has_sink, want_lse):
    rest = list(rest)
    sink_ref = rest.pop(0) if has_sink else None
    o_ref = rest.pop(0)
    lse_ref = rest.pop(0) if want_lse else None
    nc_ref, pad_ref = rest
    s_len = q_ref.shape[1]
    lb = c_ref.shape[2]
    dh = HEAD_DIM
    kvw = hk * dh
    n_row = grp * s_len
    new0 = LANES - s_len
    s_c = lax.broadcasted_iota(jnp.int32, (n_row, lb), 0) % s_len
    dist_c = lb + s_c - lax.broadcasted_iota(jnp.int32, (n_row, lb), 1)
    valid_c = (dist_c <= win) & ((dist_c & (dil - 1)) == 0)
    s_n = lax.broadcasted_iota(jnp.int32, (n_row, LANES), 0) % s_len
    c_n = lax.broadcasted_iota(jnp.int32, (n_row, LANES), 1) - new0
    dist_n = s_n - c_n
    valid_n = (c_n >= 0) & (dist_n >= 0) & (dist_n <= win) & ((dist_n & (dil - 1)) == 0)
    is_new = lax.broadcasted_iota(jnp.int32, (dh, LANES), 1) >= new0
    pad_ref[:new0, :] = jnp.zeros((new0, 2 * kvw), F32)
    for b in range(bb):
        pad_ref[new0:, :kvw] = kn_ref[b]
        pad_ref[new0:, kvw:] = vn_ref[b]
        new_t = pad_ref[...].T
        for h in range(hk):
            k_rows = slice(h * dh, (h + 1) * dh)
            v_rows = slice(kvw + h * dh, kvw + (h + 1) * dh)
            k_t = c_ref[b, k_rows, :]
            v_t = c_ref[b, v_rows, :]
            kn_t = new_t[k_rows]
            vn_t = new_t[v_rows]
            q = jnp.concatenate([q_ref[b, :, (h * grp + g) * dh:(h * grp + g + 1) * dh] for g in range(grp)],
                                axis=0)
            sc = jnp.where(valid_c, _dot(q.astype(BF16), k_t.astype(BF16)), NEG)
            sn = jnp.where(valid_n, _dot(q, kn_t), NEG)
            m = jnp.maximum(jnp.max(sc, axis=-1, keepdims=True), jnp.max(sn, axis=-1, keepdims=True))
            if has_sink:
                sk = jnp.concatenate(
                    [jnp.broadcast_to(sink_ref[:, h * grp + g:h * grp + g + 1], (s_len, 1)) for g in range(grp)],
                    axis=0)
                m = jnp.maximum(m, sk)
            pc = jnp.exp(sc - m)
            pn = jnp.exp(sn - m)
            l = jnp.sum(pc, axis=-1, keepdims=True) + jnp.sum(pn, axis=-1, keepdims=True)
            if has_sink:
                l = l + jnp.exp(sk - m)
            o = (_dot_nt(pc.astype(BF16), v_t.astype(BF16)) + _dot_nt(pn, vn_t)) / l
            for g in range(grp):
                qs = slice((h * grp + g) * dh, (h * grp + g + 1) * dh)
                o_ref[b, :, qs] = o[g * s_len:(g + 1) * s_len]
                if want_lse:
                    lse_ref[b, :, qs] = jnp.broadcast_to((m + jnp.log(l))[g * s_len:(g + 1) * s_len], (s_len, dh))
            for rows, x_t, x_new in ((k_rows, k_t, kn_t), (v_rows, v_t, vn_t)):
                moved = pltpu.roll(x_t, lb - s_len, axis=1)
                nc_ref[b, rows, :] = moved
                nc_ref[b, rows, lb - LANES:] = jnp.where(is_new, x_new, moved[:, lb - LANES:])


def _win_decode(p, n_batch, s_len, buf, q_blk, k_blk, v_blk, hk, grp, dil, win, sinks=None, want_lse=False):
    n_in = p.shape[1]
    lb = buf.shape[1]
    assert lb == win and lb % LANES == 0 and dil & (dil - 1) == 0 and s_len <= LANES
    dh = HEAD_DIM
    qw, kvw = hk * grp * dh, hk * dh
    buf_t = jnp.transpose(buf, (0, 2, 3, 4, 1)).reshape(n_batch, 2 * kvw, lb)
    bb = max(1, min(math.gcd(n_batch, 8), (2 * 1024 * 1024) // (lb * 2 * kvw * 4)))
    p3 = p.reshape(n_batch, s_len, n_in)
    (qwid, qi), (kwid, ki), (vwid, vi) = q_blk, k_blk, v_blk
    assert qwid == qw and kwid == kvw and vwid == kvw
    q_spec = pl.BlockSpec((bb, s_len, qw), lambda i: (i, 0, 0))
    buf_spec = pl.BlockSpec((bb, 2 * kvw, lb), lambda i: (i, 0, 0))
    in_specs = [
        pl.BlockSpec((bb, s_len, qw), lambda i: (i, 0, qi)),
        pl.BlockSpec((bb, s_len, kvw), lambda i: (i, 0, ki)),
        pl.BlockSpec((bb, s_len, kvw), lambda i: (i, 0, vi)),
        buf_spec,
    ]
    args = [p3, p3, p3, buf_t]
    if sinks is not None:
        in_specs.append(pl.BlockSpec((1, hk * grp), lambda i: (0, 0)))
        args.append(sinks.reshape(1, -1).astype(F32))
    q_shape = jax.ShapeDtypeStruct((n_batch, s_len, qw), F32)
    outs = pl.pallas_call(
        functools.partial(_win_decode_kernel, bb=bb, hk=hk, grp=grp, dil=dil, win=win,
                          has_sink=sinks is not None, want_lse=want_lse),
        grid=(n_batch // bb,),
        in_specs=in_specs,
        out_specs=[q_spec] * (2 if want_lse else 1) + [buf_spec],
        out_shape=[q_shape] * (2 if want_lse else 1) + [jax.ShapeDtypeStruct(buf_t.shape, F32)],
        scratch_shapes=[pltpu.VMEM((LANES, 2 * kvw), F32)],
        compiler_params=_cp("parallel"),
        name="win_decode",
    )(*args)
    nbuf = jnp.transpose(outs[-1].reshape(n_batch, 2, hk, dh, lb), (0, 4, 1, 2, 3))
    return [t.reshape(n_batch * s_len, qw) for t in outs[:-1]] + [nbuf]


def _diff_lambda(lam_ref):
    lam = lam_ref[...]
    a = jnp.sum(lam[0:1] * lam[1:2], axis=-1, keepdims=True)
    b = jnp.sum(lam[2:3] * lam[3:4], axis=-1, keepdims=True)
    return jnp.exp(a) - jnp.exp(b) + LAMBDA_INIT


def _diff_head_out(a1, a2, lam, sub_g):
    o = a1 - lam * a2
    return o * lax.rsqrt(jnp.mean(o * o, axis=-1, keepdims=True) + EPS) * sub_g * (1.0 - LAMBDA_INIT)


def _diff_prompt_kernel(q_ref, kv_ref, lam_ref, sub_ref, o_ref, qb_ref, m_ref, l_ref, acc_ref, *, tq):
    qi = pl.program_id(1)
    dh = HEAD_DIM
    dv = 2 * dh
    kw = DIFF_KV_HEADS * dv
    rows2 = DIFF_GROUP * tq
    r_i = lax.broadcasted_iota(jnp.int32, (rows2, tq), 0) % tq
    c_i = lax.broadcasted_iota(jnp.int32, (rows2, tq), 1)
    diag = c_i <= r_i
    lam = _diff_lambda(lam_ref)
    for h in range(DIFF_KV_HEADS):
        for mp in range(2):
            qb_ref[h * 2 + mp] = jnp.concatenate(
                [q_ref[:, ((h * DIFF_GROUP + g) * 2 + mp) * dh:((h * DIFF_GROUP + g) * 2 + mp + 1) * dh]
                 for g in range(DIFF_GROUP)], axis=0).astype(BF16)
    m_ref[...] = jnp.full_like(m_ref, NEG)
    l_ref[...] = jnp.zeros_like(l_ref)
    acc_ref[...] = jnp.zeros_like(acc_ref)

    def step(kb, carry, masked):
        r0 = pl.multiple_of(kb * tq, tq)
        n_map = DIFF_KV_HEADS * 2
        old = [(m_ref[c], l_ref[c], acc_ref[c]) for c in range(n_map)]
        new = []
        for h in range(DIFF_KV_HEADS):
            v = kv_ref[pl.ds(r0, tq), kw + h * dv:kw + (h + 1) * dv].astype(BF16)
            for mp in range(2):
                m_old, l_old, acc_old = old[h * 2 + mp]
                k = kv_ref[pl.ds(r0, tq), h * dv + mp * dh:h * dv + (mp + 1) * dh].astype(BF16)
                s = _dot_nt(qb_ref[h * 2 + mp], k)
                if masked:
                    s = jnp.where(diag, s, NEG)
                parts = [s[:, i * LANES:(i + 1) * LANES] for i in range(tq // LANES)]
                mx = parts[0]
                for t in parts[1:]:
                    mx = jnp.maximum(mx, t)
                m_new = jnp.maximum(m_old, jnp.max(mx, axis=-1, keepdims=True))
                corr = jnp.exp(m_old - m_new)
                ps = [jnp.exp(t - m_new) for t in parts]
                l_new = l_old * corr
                for t in ps:
                    l_new = l_new + t
                p = jnp.concatenate(ps, axis=1).astype(BF16)
                new.append((m_new, l_new, acc_old * corr + _dot(p, v)))
        for c in range(n_map):
            m_ref[c], l_ref[c], acc_ref[c] = new[c]
        return carry

    lax.fori_loop(0, qi, functools.partial(step, masked=False), 0)
    step(qi, 0, True)
    for h in range(DIFF_KV_HEADS):
        res = [acc_ref[h * 2 + mp] / jnp.sum(l_ref[h * 2 + mp], axis=-1, keepdims=True) for mp in range(2)]
        for g in range(DIFF_GROUP):
            hq = h * DIFF_GROUP + g
            o_ref[:, hq * dv:(hq + 1) * dv] = _diff_head_out(res[0][g * tq:(g + 1) * tq], res[1][g * tq:(g + 1) * tq],
                                                             lam, sub_ref[...])


def _diff_prompt(p, n_batch, seq, lam_vec, subln_g):
    n_in = p.shape[1]
    nq = DIFF_KV_HEADS * DIFF_GROUP * 2 * HEAD_DIM
    tq = math.gcd(seq, 256)
    n_map = DIFF_KV_HEADS * 2
    assert 2 * HEAD_DIM == LANES and tq % LANES == 0
    p3 = p.reshape(n_batch, seq, n_in)
    o = pl.pallas_call(
        functools.partial(_diff_prompt_kernel, tq=tq),
        grid=(n_batch, seq // tq),
        in_specs=[
            pl.BlockSpec((None, tq, nq), lambda b, i: (b, i, 0)),
            pl.BlockSpec((None, seq, nq), lambda b, i: (b, 0, 1)),
            pl.BlockSpec((4, HEAD_DIM), lambda b, i: (0, 0)),
            pl.BlockSpec((1, 2 * HEAD_DIM), lambda b, i: (0, 0)),
        ],
        out_specs=pl.BlockSpec((None, tq, nq), lambda b, i: (b, i, 0)),
        out_shape=jax.ShapeDtypeStruct((n_batch, seq, nq), F32),
        scratch_shapes=[pltpu.VMEM((n_map, DIFF_GROUP * tq, HEAD_DIM), BF16)]
        + [pltpu.VMEM((n_map, DIFF_GROUP * tq, LANES), F32)] * 3,
        compiler_params=_cp("parallel", "arbitrary"),
        name="diff_attn",
    )(p3, p3, lam_vec.astype(F32), subln_g.reshape(1, -1).astype(F32))
    return o.reshape(n_batch * seq, nq)


def _diff_sample_kernel(pt_ref, q_ref, kv_ref, lam_ref, sub_ref, *rest, pps):
    pages = rest[:pps]
    o_ref, qb_ref, m_ref, l_ref, acc_ref, nk_ref = rest[pps:]
    step = pl.program_id(1)
    n_step = pl.num_programs(1)
    s_len = q_ref.shape[0]
    dh = HEAD_DIM
    dv = 2 * dh
    kw = DIFF_KV_HEADS * dv
    hr = DIFF_GROUP * 2 * s_len
    stride = 2 * DIFF_KV_HEADS

    @pl.when(step == 0)
    def _():
        qb_ref[...] = jnp.zeros_like(qb_ref)
        for idx in range(DIFF_KV_HEADS * DIFF_GROUP * 2):
            mp = idx % 2
            qb_ref[idx * s_len:(idx + 1) * s_len, mp * dh:(mp + 1) * dh] = q_ref[:, idx * dh:(idx + 1) * dh]
        m_ref[...] = jnp.full_like(m_ref, NEG)
        l_ref[...] = jnp.zeros_like(l_ref)
        acc_ref[...] = jnp.zeros_like(acc_ref)

    def update(parts_h, v_h):
        heads = range(DIFF_KV_HEADS)
        old = [(m_ref[h * hr:(h + 1) * hr], l_ref[h * hr:(h + 1) * hr], acc_ref[h * hr:(h + 1) * hr]) for h in heads]
        new = []
        for h in heads:
            m_old, l_old, acc_old = old[h]
            parts = parts_h[h]
            mx = parts[0]
            for t in parts[1:]:
                mx = jnp.maximum(mx, t)
            m_new = jnp.maximum(m_old, jnp.max(mx, axis=-1, keepdims=True))
            corr = jnp.exp(m_old - m_new)
            ps = [jnp.exp(t - m_new) for t in parts]
            l_new = l_old * corr
            for t in ps:
                l_new = l_new + t
            p = ps[0] if len(ps) == 1 else jnp.concatenate(ps, axis=1)
            new.append((m_new, l_new, acc_old * corr + _dot(p.astype(BF16), v_h[h])))
        for h in heads:
            m_ref[h * hr:(h + 1) * hr], l_ref[h * hr:(h + 1) * hr], acc_ref[h * hr:(h + 1) * hr] = new[h]

    qhs = [qb_ref[h * hr:(h + 1) * hr, :].astype(BF16) for h in range(DIFF_KV_HEADS)]
    update([[_dot_nt(qhs[h], pg[pl.ds(h, PAGE_SIZE, stride=stride), :].astype(BF16)) for pg in pages]
            for h in range(DIFF_KV_HEADS)],
           [jnp.concatenate([pg[pl.ds(DIFF_KV_HEADS + h, PAGE_SIZE, stride=stride), :] for pg in pages],
                            axis=0).astype(BF16) for h in range(DIFF_KV_HEADS)])

    @pl.when(step == n_step - 1)
    def _():
        nk_ref[...] = jnp.zeros_like(nk_ref)
        nk_ref[:s_len, :] = kv_ref[...]
        r_s = lax.broadcasted_iota(jnp.int32, (hr, PAGE_SIZE), 0) % s_len
        c_s = lax.broadcasted_iota(jnp.int32, (hr, PAGE_SIZE), 1)
        update([[jnp.where(c_s <= r_s, _dot_nt(qhs[h], nk_ref[:, h * dv:(h + 1) * dv].astype(BF16)), NEG)]
                for h in range(DIFF_KV_HEADS)],
               [nk_ref[:, kw + h * dv:kw + (h + 1) * dv].astype(BF16) for h in range(DIFF_KV_HEADS)])
        lam = _diff_lambda(lam_ref)
        res = acc_ref[...] / jnp.sum(l_ref[...], axis=-1, keepdims=True)
        for hq in range(DIFF_KV_HEADS * DIFF_GROUP):
            a1 = res[(hq * 2) * s_len:(hq * 2 + 1) * s_len]
            a2 = res[(hq * 2 + 1) * s_len:(hq * 2 + 2) * s_len]
            o_ref[:, hq * dv:(hq + 1) * dv] = _diff_head_out(a1, a2, lam, sub_ref[...])


def _diff_sample(p, n_batch, s_len, cache, page_table, lam_vec, subln_g):
    n_in = p.shape[1]
    nq = DIFF_KV_HEADS * DIFF_GROUP * 2 * HEAD_DIM
    n_pages = page_table.shape[1]
    pps = math.gcd(n_pages, 8)
    n_rows = DIFF_KV_HEADS * DIFF_GROUP * 2 * s_len
    p3 = p.reshape(n_batch, s_len, n_in)
    kvw = 2 * DIFF_KV_HEADS * 2 * HEAD_DIM
    assert 2 * HEAD_DIM == LANES and PAGE_SIZE == LANES
    page_rows = PAGE_SIZE * 2 * DIFF_KV_HEADS
    cache3 = cache.reshape(cache.shape[0] * page_rows, LANES)

    def page_spec(jj):
        return pl.BlockSpec((page_rows, LANES), lambda b, s, pt: (pt[b * n_pages + s * pps + jj], 0))

    grid_spec = pltpu.PrefetchScalarGridSpec(
        num_scalar_prefetch=1,
        grid=(n_batch, n_pages // pps),
        in_specs=[
            pl.BlockSpec((None, s_len, nq), lambda b, s, pt: (b, 0, 0)),
            pl.BlockSpec((None, s_len, nq), lambda b, s, pt: (b, 0, 1)),
            pl.BlockSpec((4, HEAD_DIM), lambda b, s, pt: (0, 0)),
            pl.BlockSpec((1, 2 * HEAD_DIM), lambda b, s, pt: (0, 0)),
        ] + [page_spec(jj) for jj in range(pps)],
        out_specs=pl.BlockSpec((None, s_len, nq), lambda b, s, pt: (b, 0, 0)),
        scratch_shapes=[pltpu.VMEM((n_rows, LANES), F32)] * 4 + [pltpu.VMEM((PAGE_SIZE, kvw), F32)],
    )
    o = pl.pallas_call(
        functools.partial(_diff_sample_kernel, pps=pps),
        grid_spec=grid_spec,
        out_shape=jax.ShapeDtypeStruct((n_batch, s_len, nq), F32),
        compiler_params=_cp("parallel", "arbitrary"),
        name="diff_decode",
    )(page_table.reshape(-1), p3, p3, lam_vec.astype(F32), subln_g.reshape(1, -1).astype(F32),
      *([cache3] * pps))
    return o.reshape(n_batch * s_len, nq)


def kernel(x_prompt, x_sample, state_gla, cache_swa, cache_dil1, cache_dil2, cache_dil3, cache_diff, page_table, c_prompt, c_sample, norm1_g, norm2_g, ada_w, ada_b, gla_w_in, gla_w_gate2, gla_b_gate, gla_onorm_g, gla_w_out, swa_w_in, swa_q_norm, swa_k_norm, swa_sinks, swa_w_out, dil_w_in, dil_q_norm, dil_k_norm, dil_w_out, diff_w_in, diff_q_norm, diff_k_norm, diff_lambda, diff_subln_g, diff_w_out, ffn_w_gu, ffn_w_down, moe_router, moe_w_gu, moe_w_down):
    bp, seq, d = x_prompt.shape
    db, s_len, _ = x_sample.shape
    depth = ada_w.shape[0]
    dh = HEAD_DIM
    qscale = dh ** -0.5

    mods = _ada(jnp.concatenate([c_prompt, c_sample], axis=0), ada_w, ada_b)
    xp = x_prompt.reshape(bp * seq, d)
    xs = x_sample.reshape(db * s_len, d)
    out = {}

    for i in range(depth):
        gp = _Group(bp, seq, mods[i, :bp])
        gs = _Group(db, s_len, mods[i, bp:])
        kind = i % 4
        if kind == 0:
            n_in = gla_w_in.shape[1]
            n_pad = -(-n_in // (5 * LANES)) * (5 * LANES)
            w_in = jnp.pad(gla_w_in, ((0, 0), (0, n_pad - n_in)))
            lr_w = n_pad - (n_in - GLA_RANK)
            wg2 = jnp.pad(gla_w_gate2, ((0, lr_w - GLA_RANK), (0, 0)))
            pp = _inproj(gp, xp, norm1_g[i], 0, w_in, n_pad // 5)
            ps = _inproj(gs, xs, norm1_g[i], 0, w_in, n_pad // 5)
            op, out["gla_p"] = _gla(pp, bp, seq, wg2, gla_b_gate, gla_onorm_g, None)
            os_, out["gla_s"] = _gla(ps, db, s_len, wg2, gla_b_gate, gla_onorm_g, state_gla)
            xp = _outproj(gp, xp, op, gla_w_out, 2)
            xs = _outproj(gs, xs, os_, gla_w_out, 2)
        elif kind == 1:
            hk = SWA_KV_HEADS
            nq = swa_w_out.shape[0]
            grp = nq // dh // hk
            kvw = hk * dh
            norm = _norm_rows([(nq, swa_q_norm, qscale), (kvw, swa_k_norm, 1.0), (kvw, None, 1.0)])
            pp = _inproj(gp, xp, norm1_g[i], 0, swa_w_in, kvw * 2, norm)
            ps = _inproj(gs, xs, norm1_g[i], 0, swa_w_in, kvw * 2, norm)
            (op,) = _banded(pp, bp, seq, 1, (nq, 0), (kvw, nq // kvw), (kvw, nq // kvw + 1),
                            hk, grp, SWA_WINDOW, sinks=swa_sinks)
            keep = min(SWA_WINDOW, seq)
            out["swa_p"] = _kv_tail(pp, bp, seq, nq, hk, keep)
            os_, out["swa_s"] = _win_decode(ps, db, s_len, cache_swa, (nq, 0), (kvw, nq // kvw),
                                            (kvw, nq // kvw + 1), hk, grp, 1, SWA_WINDOW, sinks=swa_sinks)
            xp = _outproj(gp, xp, op, swa_w_out, 2)
            xs = _outproj(gs, xs, os_, swa_w_out, 2)
        elif kind == 2:
            w = DIL_HEADS * dh
            norm = _norm_rows([(w, dil_q_norm, qscale), (w, dil_k_norm, 1.0), (w, None, 1.0)] * len(DIL_GROUPS))
            pp = _inproj(gp, xp, norm1_g[i], 0, dil_w_in, w, norm)
            ps = _inproj(gs, xs, norm1_g[i], 0, dil_w_in, w, norm)
            outs_p, lses_p, outs_s, lses_s = [], [], [], []
            caches = (cache_dil1, cache_dil2, cache_dil3)
            for gi, (win, dil) in enumerate(DIL_GROUPS):
                o, lse = _banded(pp, bp, seq, dil, (w, 3 * gi), (w, 3 * gi + 1), (w, 3 * gi + 2),
                                 DIL_HEADS, 1, win // dil, want_lse=True)
                outs_p.append(o)
                lses_p.append(lse)
                keep = min(win, seq)
                out["dil%d_p" % gi] = _kv_tail(pp, bp, seq, (3 * gi + 1) * w, DIL_HEADS, keep)
                o, lse, out["dil%d_s" % gi] = _win_decode(ps, db, s_len, caches[gi], (w, 3 * gi), (w, 3 * gi + 1),
                                                          (w, 3 * gi + 2), DIL_HEADS, 1, dil, win, want_lse=True)
                outs_s.append(o)
                lses_s.append(lse)
            xp = _outproj_mix(gp, xp, outs_p, lses_p, dil_w_out, 2)
            xs = _outproj_mix(gs, xs, outs_s, lses_s, dil_w_out, 2)
        else:
            nq = DIFF_KV_HEADS * DIFF_GROUP * 2 * dh
            nk = DIFF_KV_HEADS * 2 * dh
            norm = _norm_rows([(nq, diff_q_norm, qscale), (nk, diff_k_norm, 1.0), (nk, None, 1.0)])
            pp = _inproj(gp, xp, norm1_g[i], 0, diff_w_in, nk, norm)
            ps = _inproj(gs, xs, norm1_g[i], 0, diff_w_in, nk, norm)
            op = _diff_prompt(pp, bp, seq, diff_lambda, diff_subln_g)
            os_ = _diff_sample(ps, db, s_len, cache_diff, page_table, diff_lambda, diff_subln_g)
            out["diff_p"] = pp[:, nq:].reshape(bp, seq, 2, DIFF_KV_HEADS, 2 * dh)
            out["diff_s"] = ps[:, nq:].reshape(db, s_len, 2, DIFF_KV_HEADS, 2 * dh)
            xp = _outproj(gp, xp, op, diff_w_out, 2)
            xs = _outproj(gs, xs, os_, diff_w_out, 2)
        j = i // 2
        if i % 2 == 0:
            w_gu, w_down = ffn_w_gu[j].astype(BF16), ffn_w_down[j].astype(BF16)
            xp = _ffn(gp, xp, norm2_g[i], w_gu, w_down, 1408)
            xs = _ffn(gs, xs, norm2_g[i], w_gu, w_down, 1408)
        else:
            w_gu, w_down = moe_w_gu[j].astype(BF16), moe_w_down[j].astype(BF16)
            xp = _moe(gp, xp, norm2_g[i], moe_router[j], w_gu, w_down, 896)
            xs = _moe(gs, xs, norm2_g[i], moe_router[j], w_gu, w_down, 896)

    return (xp.reshape(bp, seq, d), xs.reshape(db, s_len, d), out["gla_p"], out["gla_s"],
            out["swa_p"], out["swa_s"], out["dil0_p"], out["dil0_s"], out["dil1_p"], out["dil1_s"],
            out["dil2_p"], out["dil2_s"], out["diff_p"], out["diff_s"])
```

```python
import functools
import math

import jax
import jax.numpy as jnp
from jax import lax
from jax.experimental import pallas as pl
from jax.experimental.pallas import tpu as pltpu

F32 = jnp.float32
BF16 = jnp.bfloat16

EPS = 1e-6
NEG = -1e30
BLK = 128
HEAD_DIM = 64

GLA_HEADS = 4
GLA_RANK = 16
GLA_NORMALIZER = 16.0
GLA_CHUNK = 64

SWA_KV_HEADS = 4
SWA_WINDOW = 128
DIL_GROUPS = ((128, 1), (512, 4), (2048, 16))
DIL_HEADS = 8
DIFF_KV_HEADS = 4
DIFF_GROUP = 2
LAMBDA_INIT = 0.8 - 0.6 * math.exp(-0.3 * 3)
N_EXPERTS = 8
PAGE_SIZE = 128

LANES = 128
VMEM_LIMIT = 56 * 1024 * 1024


def _cp(*sem):
    return pltpu.CompilerParams(dimension_semantics=sem, vmem_limit_bytes=VMEM_LIMIT)


def _dot(a, b):
    return jnp.dot(a, b, preferred_element_type=F32)


def _dot_nt(a, b):
    return lax.dot_general(a, b, (((1,), (1,)), ((), ())), preferred_element_type=F32)


def _dot_tn(a, b):
    return lax.dot_general(a, b, (((0,), (0,)), ((), ())), preferred_element_type=F32)


def _silu(x):
    return x / (1.0 + jnp.exp(-x))


def _adaln(x, g, shift, scale):
    y = x * lax.rsqrt(jnp.mean(x * x, axis=-1, keepdims=True) + EPS)
    return y * g * (1.0 + scale) + shift


class _Group:
    def __init__(self, n_batch, rows_per_batch, mods):
        self.m = n_batch * rows_per_batch
        self.rows_per_batch = rows_per_batch
        self.prompt = rows_per_batch >= 256
        d6 = mods.shape[-1]
        if self.prompt:
            self.mods = mods.reshape(n_batch, 1, d6)
        else:
            self.mods = jnp.repeat(mods, rows_per_batch, axis=0)
        self.d = d6 // 6
        self.tm = self.tile(1024)

    def tile(self, tm_max):
        return math.gcd(self.rows_per_batch if self.prompt else self.m, tm_max)

    def mod_spec(self, k, tm=None):
        d = self.d
        tm = tm or self.tm
        if self.prompt:
            tpb = self.rows_per_batch // tm
            return pl.BlockSpec((None, 1, d), lambda i, *_: (i // tpb, 0, k))
        return pl.BlockSpec((tm, d), lambda i, *_: (i, k))


def _ada_kernel(c_ref, w_ref, b_ref, o_ref):
    a = _silu(c_ref[...]).astype(BF16)
    o_ref[...] = _dot(a, w_ref[...].astype(BF16)) + b_ref[...]


def _ada(c_all, ada_w, ada_b):
    depth, d, d6 = ada_w.shape
    nb = c_all.shape[0]
    tn = 1024
    return pl.pallas_call(
        _ada_kernel,
        grid=(depth, d6 // tn),
        in_specs=[
            pl.BlockSpec((nb, d), lambda l, j: (0, 0)),
            pl.BlockSpec((None, d, tn), lambda l, j: (l, 0, j)),
            pl.BlockSpec((None, 1, tn), lambda l, j: (l, 0, j)),
        ],
        out_specs=pl.BlockSpec((None, nb, tn), lambda l, j: (l, 0, j)),
        out_shape=jax.ShapeDtypeStruct((depth, nb, d6), F32),
        compiler_params=_cp("parallel", "parallel"),
        name="ada_mod",
    )(c_all, ada_w, ada_b.reshape(depth, 1, d6))


def _inproj_kernel(x_ref, g_ref, sh_ref, sc_ref, w_ref, *rest, norm):
    if norm:
        flag_ref, gain_ref, o_ref, h_ref = rest
    else:
        o_ref, h_ref = rest
    j = pl.program_id(1)

    @pl.when(j == 0)
    def _():
        h_ref[...] = _adaln(x_ref[...], g_ref[...], sh_ref[...], sc_ref[...]).astype(BF16)

    acc = _dot(h_ref[...], w_ref[...].astype(BF16))
    if not norm:
        o_ref[...] = acc
        return
    tn = acc.shape[1]
    lo = lax.broadcasted_iota(jnp.int32, (1, LANES), 1) < HEAD_DIM
    for c in range(tn // LANES):
        sl = slice(c * LANES, (c + 1) * LANES)
        a = acc[:, sl]
        sq = a * a
        s_lo = jnp.sum(jnp.where(lo, sq, 0.0), axis=-1, keepdims=True)
        s_hi = jnp.sum(jnp.where(lo, 0.0, sq), axis=-1, keepdims=True)
        r = lax.rsqrt(jnp.where(lo, s_lo, s_hi) * (1.0 / HEAD_DIM) + EPS)
        o_ref[:, sl] = a * jnp.where(flag_ref[:, sl] > 0.0, r, 1.0) * gain_ref[:, sl]


def _inproj(grp, x, norm_g, k_shift, w, tn, norm=None):
    m, d = x.shape
    n = w.shape[1]
    tm = grp.tm
    in_specs = [
        pl.BlockSpec((tm, d), lambda i, j: (i, 0)),
        pl.BlockSpec((1, d), lambda i, j: (0, 0)),
        grp.mod_spec(k_shift),
        grp.mod_spec(k_shift + 1),
        pl.BlockSpec((d, tn), lambda i, j: (0, j)),
    ]
    args = [x, norm_g.reshape(1, d), grp.mods, grp.mods, w]
    if norm is not None:
        in_specs += [pl.BlockSpec((1, tn), lambda i, j: (0, j))] * 2
        args += list(norm)
    return pl.pallas_call(
        functools.partial(_inproj_kernel, norm=norm is not None),
        grid=(m // tm, n // tn),
        in_specs=in_specs,
        out_specs=pl.BlockSpec((tm, tn), lambda i, j: (i, j)),
        out_shape=jax.ShapeDtypeStruct((m, n), F32),
        scratch_shapes=[pltpu.VMEM((tm, d), BF16)],
        compiler_params=_cp("parallel", "arbitrary"),
        name="adaln_inproj",
    )(*args)


def _norm_rows(pieces):
    flags, gains = [], []
    for width, g, scale in pieces:
        if g is None:
            flags.append(jnp.zeros((width,), F32))
            gains.append(jnp.ones((width,), F32))
        else:
            flags.append(jnp.ones((width,), F32))
            gains.append(jnp.tile(g.astype(F32) * scale, width // HEAD_DIM))
    return jnp.concatenate(flags)[None, :], jnp.concatenate(gains)[None, :]


def _outproj_kernel(x_ref, o_ref, gt_ref, w_ref, out_ref):
    y = _dot(o_ref[...].astype(BF16), w_ref[...].astype(BF16))
    out_ref[...] = x_ref[...] + gt_ref[...] * y


def _outproj(grp, x, o, w, k_gate):
    m, d = x.shape
    kdim = w.shape[0]
    tm = grp.tm
    return pl.pallas_call(
        _outproj_kernel,
        grid=(m // tm,),
        in_specs=[
            pl.BlockSpec((tm, d), lambda i: (i, 0)),
            pl.BlockSpec((tm, kdim), lambda i: (i, 0)),
            grp.mod_spec(k_gate),
            pl.BlockSpec((kdim, d), lambda i: (0, 0)),
        ],
        out_specs=pl.BlockSpec((tm, d), lambda i: (i, 0)),
        out_shape=jax.ShapeDtypeStruct((m, d), F32),
        compiler_params=_cp("parallel"),
        name="outproj",
    )(x, o, grp.mods, w)


def _outproj_mix_kernel(x_ref, o0_ref, o1_ref, o2_ref, l0_ref, l1_ref, l2_ref, gt_ref, w_ref, out_ref):
    l0, l1, l2 = l0_ref[...], l1_ref[...], l2_ref[...]
    mx = jnp.maximum(jnp.maximum(l0, l1), l2)
    e0, e1, e2 = jnp.exp(l0 - mx), jnp.exp(l1 - mx), jnp.exp(l2 - mx)
    o = (e0 * o0_ref[...] + e1 * o1_ref[...] + e2 * o2_ref[...]) / (e0 + e1 + e2)
    y = _dot(o.astype(BF16), w_ref[...].astype(BF16))
    out_ref[...] = x_ref[...] + gt_ref[...] * y


def _outproj_mix(grp, x, outs, lses, w, k_gate):
    m, d = x.shape
    kdim = w.shape[0]
    tm = grp.tm
    row = pl.BlockSpec((tm, kdim), lambda i: (i, 0))
    return pl.pallas_call(
        _outproj_mix_kernel,
        grid=(m // tm,),
        in_specs=[pl.BlockSpec((tm, d), lambda i: (i, 0))] + [row] * 6
        + [grp.mod_spec(k_gate), pl.BlockSpec((kdim, d), lambda i: (0, 0))],
        out_specs=pl.BlockSpec((tm, d), lambda i: (i, 0)),
        out_shape=jax.ShapeDtypeStruct((m, d), F32),
        compiler_params=_cp("parallel"),
        name="outproj_mix",
    )(x, *outs, *lses, grp.mods, w)


def _ffn_kernel(x_ref, g_ref, sh_ref, sc_ref, gt_ref, wg_ref, wu_ref, wd_ref, o_ref, h_ref):
    f = pl.program_id(1)
    nf = pl.num_programs(1)

    @pl.when(f == 0)
    def _():
        h_ref[...] = _adaln(x_ref[...], g_ref[...], sh_ref[...], sc_ref[...]).astype(BF16)
        o_ref[...] = jnp.zeros_like(o_ref)

    h = h_ref[...]
    gate = _dot(h, wg_ref[...])
    up = _dot(h, wu_ref[...])
    a = (_silu(gate) * up).astype(BF16)
    o_ref[...] += _dot(a, wd_ref[...])

    @pl.when(f == nf - 1)
    def _():
        o_ref[...] = x_ref[...] + gt_ref[...] * o_ref[...]


def _ffn(grp, x, norm_g, w_gu, w_down, tf):
    m, d = x.shape
    dff = w_down.shape[0]
    tm = grp.tm
    nf = dff // tf
    return pl.pallas_call(
        _ffn_kernel,
        grid=(m // tm, nf),
        in_specs=[
            pl.BlockSpec((tm, d), lambda i, f: (i, 0)),
            pl.BlockSpec((1, d), lambda i, f: (0, 0)),
            grp.mod_spec(3), grp.mod_spec(4), grp.mod_spec(5),
            pl.BlockSpec((d, tf), lambda i, f: (0, f)),
            pl.BlockSpec((d, tf), lambda i, f: (0, f + nf)),
            pl.BlockSpec((tf, d), lambda i, f: (f, 0)),
        ],
        out_specs=pl.BlockSpec((tm, d), lambda i, f: (i, 0)),
        out_shape=jax.ShapeDtypeStruct((m, d), F32),
        scratch_shapes=[pltpu.VMEM((tm, d), BF16)],
        compiler_params=_cp("parallel", "arbitrary"),
        name="ffn_swiglu",
    )(x, norm_g.reshape(1, d), grp.mods, grp.mods, grp.mods, w_gu, w_gu, w_down)


def _route_top2(logits):
    lane = lax.broadcasted_iota(jnp.int32, logits.shape, 1)
    lg = jnp.where(lane < N_EXPERTS, logits, -jnp.inf)
    m1 = jnp.max(lg, axis=-1, keepdims=True)
    i1 = jnp.min(jnp.where(lg == m1, lane, LANES), axis=-1, keepdims=True)
    lg2 = jnp.where(lane == i1, -jnp.inf, lg)
    m2 = jnp.max(lg2, axis=-1, keepdims=True)
    i2 = jnp.min(jnp.where(lg2 == m2, lane, LANES), axis=-1, keepdims=True)
    e2 = jnp.exp(m2 - m1)
    g1 = 1.0 / (1.0 + e2)
    g2 = e2 / (1.0 + e2)
    comb = jnp.where(lane == i1, g1, 0.0) + jnp.where(lane == i2, g2, 0.0)
    return comb, (lane == i1) | (lane == i2)


def _moe_kernel(x_ref, g_ref, sh_ref, sc_ref, gt_ref, r_ref, wg_ref, wu_ref, wd_ref, o_ref,
                h_ref, comb_ref, key_ref, keyt_ref, cnt_ref, xs_ref, acc_ref, n_ref, *, sub):
    e = pl.program_id(1)
    f = pl.program_id(2)
    nf = pl.num_programs(2)
    tm = h_ref.shape[0]
    lane = lax.broadcasted_iota(jnp.int32, (tm, LANES), 1)

    @pl.when((e == 0) & (f == 0))
    def _():
        x = x_ref[...]
        h = _adaln(x, g_ref[...], sh_ref[...], sc_ref[...]).astype(BF16)
        h_ref[...] = h
        comb, sel = _route_top2(_dot(h, r_ref[...].astype(BF16)))
        comb_ref[...] = comb
        tb = math.gcd(tm, 256)
        tri = jnp.where(lax.broadcasted_iota(jnp.int32, (tb, tb), 0) >= lax.broadcasted_iota(jnp.int32, (tb, tb), 1),
                        1.0, 0.0).astype(BF16)
        carry = jnp.zeros((1, LANES), F32)
        for blk in range(tm // tb):
            rows = slice(blk * tb, (blk + 1) * tb)
            sb = jnp.where(sel[rows], 1.0, 0.0)
            incl = _dot(tri, sb.astype(BF16)) + carry
            key_ref[rows, :] = jnp.where(sb > 0.0, incl - 1.0, -1.0).astype(jnp.int32)
            carry = carry + jnp.sum(sb, axis=0, keepdims=True)
        cnt_ref[...] = carry
        keyt_ref[...] = key_ref[...].T
        o_ref[...] = x

    def onehot(r0, nr):
        keyrow = keyt_ref[pl.ds(e, 1), :]
        r = lax.broadcasted_iota(jnp.int32, (nr, tm), 0) + r0
        return jnp.where(keyrow == r, 1.0, 0.0).astype(BF16)

    def for_blocks(fn):
        def body(s, carry):
            fn(pl.multiple_of(s * sub, sub), sub)
            return carry
        lax.fori_loop(0, (n_e + sub - 1) // sub, body, 0)

    @pl.when(f == 0)
    def _():
        cnt = jnp.sum(jnp.where(lane[:1] == e, cnt_ref[...], 0.0), axis=-1, keepdims=True)
        n_ref[0] = cnt.astype(jnp.int32)[0, 0]

    n_e = n_ref[0]

    @pl.when(f == 0)
    def _():
        def pack(r0, nr):
            xs_ref[pl.ds(r0, nr), :] = _dot(onehot(r0, nr), h_ref[...]).astype(BF16)
            acc_ref[pl.ds(r0, nr), :] = jnp.zeros((nr, acc_ref.shape[1]), F32)
        for_blocks(pack)

    def expert(r0, nr):
        xb = xs_ref[pl.ds(r0, nr), :]
        a = (_silu(_dot(xb, wg_ref[...].astype(BF16))) * _dot(xb, wu_ref[...].astype(BF16))).astype(BF16)
        acc_ref[pl.ds(r0, nr), :] += _dot(a, wd_ref[...].astype(BF16))
    for_blocks(expert)

    @pl.when(f == nf - 1)
    def _():
        ce = jnp.sum(jnp.where(lane == e, comb_ref[...], 0.0), axis=-1, keepdims=True)
        c_hi = ce.astype(BF16).astype(F32)
        c_mid = (ce - c_hi).astype(BF16).astype(F32)
        c_lo = ce - c_hi - c_mid
        cw = jnp.where(lane == 0, c_hi, jnp.where(lane == 1, c_mid, jnp.where(lane == 2, c_lo, 0.0))).astype(BF16)
        kcol = jnp.sum(jnp.where(lane == e, key_ref[...].astype(F32), 0.0), axis=-1,
                       keepdims=True).astype(jnp.int32)

        def unpack(r0, nr):
            g3 = _dot(onehot(r0, nr), cw)
            gate = g3[:, 0:1] + g3[:, 1:2] + g3[:, 2:3]
            y = (acc_ref[pl.ds(r0, nr), :] * gate).astype(BF16)
            c = lax.broadcasted_iota(jnp.int32, (tm, nr), 1) + r0
            scatter = jnp.where(kcol == c, 1.0, 0.0).astype(BF16)
            o_ref[...] += gt_ref[...] * _dot(scatter, y)
        for_blocks(unpack)


def _moe(grp, x, norm_g, router, w_gu, w_down, tf):
    m, d = x.shape
    ne, dff, _ = w_down.shape
    tm = grp.tile(2048)
    sub = math.gcd(tm, 256)
    nf = dff // tf
    router_p = jnp.pad(router, ((0, 0), (0, LANES - ne)))
    resident = dict(pipeline_mode=pl.Buffered(1))
    return pl.pallas_call(
        functools.partial(_moe_kernel, sub=sub),
        grid=(m // tm, ne, nf),
        in_specs=[
            pl.BlockSpec((tm, d), lambda i, e, f: (i, 0), **resident),
            pl.BlockSpec((1, d), lambda i, e, f: (0, 0)),
            grp.mod_spec(3, tm), grp.mod_spec(4, tm), grp.mod_spec(5, tm),
            pl.BlockSpec((d, LANES), lambda i, e, f: (0, 0)),
            pl.BlockSpec((None, d, tf), lambda i, e, f: (e, 0, f)),
            pl.BlockSpec((None, d, tf), lambda i, e, f: (e, 0, f + nf)),
            pl.BlockSpec((None, tf, d), lambda i, e, f: (e, f, 0)),
        ],
        out_specs=pl.BlockSpec((tm, d), lambda i, e, f: (i, 0), **resident),
        out_shape=jax.ShapeDtypeStruct((m, d), F32),
        scratch_shapes=[
            pltpu.VMEM((tm, d), BF16), pltpu.VMEM((tm, LANES), F32), pltpu.VMEM((tm, LANES), jnp.int32),
            pltpu.VMEM((LANES, tm), jnp.int32), pltpu.VMEM((1, LANES), F32),
            pltpu.VMEM((tm, d), BF16), pltpu.VMEM((tm, d), F32),
            pltpu.SMEM((1,), jnp.int32),
        ],
        compiler_params=_cp("parallel", "arbitrary", "arbitrary"),
        name="moe",
    )(x, norm_g.reshape(1, d), grp.mods, grp.mods, grp.mods, router_p, w_gu, w_gu, w_down)


def _gla_kernel(p_ref, wg2_ref, bg_ref, on_ref, *rest, chunk, n_chunk, n_valid, has_s0):
    rest = list(rest)
    s0_ref = rest.pop(0) if has_s0 else None
    o_ref, sfin_ref, s_ref = rest[:3]
    if n_valid < chunk:
        src = rest[3]
        src[...] = jnp.zeros_like(src)
        src[:n_valid, :] = p_ref[...]
    else:
        src = p_ref
    li = pl.program_id(1)
    nl = pl.num_programs(1)
    nq = GLA_HEADS * 128
    nv = GLA_HEADS * 256
    dk, dv = 128, 256

    @pl.when(li == 0)
    def _():
        if has_s0:
            s_ref[...] = s0_ref[...]
        else:
            s_ref[...] = jnp.zeros_like(s_ref)

    c = chunk
    row = lax.broadcasted_iota(jnp.int32, (c, c), 0)
    col = lax.broadcasted_iota(jnp.int32, (c, c), 1)
    causal = row >= col
    tri = jnp.where(causal, 1.0, 0.0).astype(BF16)
    ones = jnp.ones((c, dv), BF16)
    live = None
    if n_valid < c:
        live = lax.broadcasted_iota(jnp.int32, (c, 1), 0) < n_valid
    for ci in range(n_chunk):
        rows = slice(ci * c, (ci + 1) * c)
        a_lr = src[rows, 2 * nq + 2 * nv:]
        z = _dot(a_lr.astype(BF16), wg2_ref[...].astype(BF16)) + bg_ref[...]
        la = (jnp.minimum(z, 0.0) - jnp.log1p(jnp.exp(-jnp.abs(z)))) * (1.0 / GLA_NORMALIZER)
        if live is not None:
            la = jnp.where(live, la, 0.0)
        hi = la.astype(BF16)
        lo = (la - hi.astype(F32)).astype(BF16)
        b = _dot(tri, hi) + _dot(tri, lo)
        for h in range(GLA_HEADS):
            ks = slice(h * dk, (h + 1) * dk)
            vs = slice(h * dv, (h + 1) * dv)
            bh = b[:, ks]
            bl = bh[c - 1:c, :]
            qh = src[rows, h * dk:(h + 1) * dk] * (dk ** -0.5)
            kh = src[rows, nq + h * dk:nq + (h + 1) * dk]
            vh = src[rows, 2 * nq + h * dv:2 * nq + (h + 1) * dv]
            gh = src[rows, 2 * nq + nv + h * dv:2 * nq + nv + (h + 1) * dv]
            vb = vh.astype(BF16)
            qd = (qh * jnp.exp(bh)).astype(BF16)
            ki = (kh * jnp.exp(-bh)).astype(BF16)
            kd = (kh * jnp.exp(bl - bh)).astype(BF16)
            att = jnp.where(causal, _dot_nt(qd, ki), 0.0).astype(BF16)
            s = s_ref[h]
            o = _dot(att, vb) + _dot(qd, s.astype(BF16))
            dl = _dot_tn(hi[:, ks], ones) + _dot_tn(lo[:, ks], ones)
            s_ref[h] = jnp.exp(dl) * s + _dot_tn(kd, vb)
            on = o * lax.rsqrt(jnp.mean(o * o, axis=-1, keepdims=True) + EPS) * on_ref[...]
            res = on * _silu(gh)
            if n_valid < c:
                o_ref[:, vs] = res[:n_valid]
            else:
                o_ref[rows, vs] = res

    @pl.when(li == nl - 1)
    def _():
        sfin_ref[...] = s_ref[...]


def _gla(p, n_batch, seq, w_gate2_p, b_gate, onorm_g, s0):
    n_in = p.shape[1]
    nv = GLA_HEADS * 256
    c_ref = math.gcd(seq, GLA_CHUNK)
    if seq >= GLA_CHUNK:
        chunk, rows, n_valid = c_ref, math.gcd(seq, 256), c_ref
        scratch = []
    else:
        chunk, rows, n_valid = GLA_CHUNK, seq, seq
        scratch = [pltpu.VMEM((chunk, n_in), F32)]
    n_l = seq // rows
    p3 = p.reshape(n_batch, seq, n_in)
    has_s0 = s0 is not None
    st_spec = pl.BlockSpec((None, GLA_HEADS, 128, 256), lambda b, l: (b, 0, 0, 0))
    in_specs = [
        pl.BlockSpec((None, rows, n_in), lambda b, l: (b, l, 0)),
        pl.BlockSpec((LANES, GLA_HEADS * 128), lambda b, l: (0, 0)),
        pl.BlockSpec((1, GLA_HEADS * 128), lambda b, l: (0, 0)),
        pl.BlockSpec((1, 256), lambda b, l: (0, 0)),
    ]
    args = [p3, w_gate2_p, b_gate.reshape(1, -1), onorm_g.reshape(1, -1)]
    if has_s0:
        in_specs.append(st_spec)
        args.append(s0)
    o, s_fin = pl.pallas_call(
        functools.partial(_gla_kernel, chunk=chunk, n_chunk=max(rows // chunk, 1), n_valid=n_valid,
                          has_s0=has_s0),
        grid=(n_batch, n_l),
        in_specs=in_specs,
        out_specs=[pl.BlockSpec((None, rows, nv), lambda b, l: (b, l, 0)), st_spec],
        out_shape=[jax.ShapeDtypeStruct((n_batch, seq, nv), F32),
                   jax.ShapeDtypeStruct((n_batch, GLA_HEADS, 128, 256), F32)],
        scratch_shapes=[pltpu.VMEM((GLA_HEADS, 128, 256), F32)] + scratch,
        compiler_params=_cp("parallel", "arbitrary"),
        name="gla",
    )(*args)
    return o.reshape(n_batch * seq, nv), s_fin


def _banded_kernel(q_ref, kc_ref, kp_ref, vc_ref, vp_ref, *rest, hk, grp, dil, window, has_sink, want_lse):
    rest = list(rest)
    sink_ref = rest.pop(0) if has_sink else None
    o_ref = rest.pop(0)
    lse_ref = rest.pop(0) if want_lse else None
    n = pl.program_id(1)
    i = lax.broadcasted_iota(jnp.int32, (2 * BLK, BLK), 0) & (BLK - 1)
    j = lax.broadcasted_iota(jnp.int32, (2 * BLK, BLK), 1)
    valid_c = (j <= i) & (i - j <= window)
    valid_p = i - j + BLK + jnp.where(n > 0, 0, 2 * BLK) <= window
    lo = lax.broadcasted_iota(jnp.int32, (BLK, LANES), 1) < HEAD_DIM
    top = lax.broadcasted_iota(jnp.int32, (2 * BLK, 1), 0) < BLK
    n_tile = q_ref.shape[1] // LANES
    for r in range(dil):
        rows = pl.ds(r, BLK, stride=dil) if dil > 1 else slice(None)
        for t in range(n_tile):
            kv_head = 2 * t // grp
            cols = slice((kv_head // 2) * LANES, (kv_head // 2 + 1) * LANES)
            kv = [ref[rows, cols] for ref in (kc_ref, kp_ref, vc_ref, vp_ref)]
            if grp > 1:
                own_lo = kv_head % 2 == 0
                kv = [jnp.where(lo, x if own_lo else pltpu.roll(x, HEAD_DIM, axis=1),
                                pltpu.roll(x, HEAD_DIM, axis=1) if own_lo else x) for x in kv]
            kc, kp, vc, vp = [x.astype(BF16) for x in kv]
            q2 = q_ref[rows, t * LANES:(t + 1) * LANES]
            qbd = jnp.concatenate([jnp.where(lo, q2, 0.0), jnp.where(lo, 0.0, q2)], axis=0).astype(BF16)
            sc = jnp.where(valid_c, _dot_nt(qbd, kc), NEG)
            sp = jnp.where(valid_p, _dot_nt(qbd, kp), NEG)
            m = jnp.max(jnp.maximum(sc, sp), axis=-1, keepdims=True)
            if has_sink:
                sk = jnp.where(top, sink_ref[:, 2 * t:2 * t + 1], sink_ref[:, 2 * t + 1:2 * t + 2])
                m = jnp.maximum(m, sk)
            pc = jnp.exp(sc - m)
            pp = jnp.exp(sp - m)
            l = jnp.sum(pc + pp, axis=-1, keepdims=True)
            if has_sink:
                l = l + jnp.exp(sk - m)
            pv = (_dot(pc.astype(BF16), vc) + _dot(pp.astype(BF16), vp)) / l
            o_ref[rows, t * LANES:(t + 1) * LANES] = jnp.where(lo, pv[:BLK], pv[BLK:])
            if want_lse:
                lse = m + jnp.log(l)
                lse_ref[rows, t * LANES:(t + 1) * LANES] = jnp.where(
                    lo, jnp.broadcast_to(lse[:BLK], (BLK, LANES)), jnp.broadcast_to(lse[BLK:], (BLK, LANES)))


def _banded(p, n_batch, seq, dil, q_blk, k_blk, v_blk, hk, grp, window, sinks=None, want_lse=False):
    n_in = p.shape[1]
    rows = dil * BLK
    assert seq % rows == 0
    nb = seq // rows
    p3 = p.reshape(n_batch, seq, n_in)
    qw, qi = q_blk
    kw, ki = k_blk
    vw, vi = v_blk
    if dil > 1:
        assert grp == 1 and qw == kw == vw and qw % LANES == 0
        n_cb = qw // LANES
        qi, ki, vi = qi * n_cb, ki * n_cb, vi * n_cb
        qw = kw = vw = LANES
        hk = LANES // HEAD_DIM
    else:
        n_cb = 1

    def cur(w, ci):
        return pl.BlockSpec((None, rows, w), lambda b, n, c: (b, n, ci + c))

    def prev(w, ci):
        return pl.BlockSpec((None, rows, w), lambda b, n, c: (b, jnp.maximum(n - 1, 0), ci + c))

    in_specs = [cur(qw, qi), cur(kw, ki), prev(kw, ki), cur(vw, vi), prev(vw, vi)]
    args = [p3] * 5
    if sinks is not None:
        in_specs.append(pl.BlockSpec((1, sinks.shape[0]), lambda b, n, c: (0, 0)))
        args.append(sinks.reshape(1, -1).astype(F32))
    o_spec = pl.BlockSpec((None, rows, qw), lambda b, n, c: (b, n, c))
    o_shape = jax.ShapeDtypeStruct((n_batch, seq, n_cb * qw), F32)
    outs = pl.pallas_call(
        functools.partial(_banded_kernel, hk=hk, grp=grp, dil=dil, window=window,
                          has_sink=sinks is not None, want_lse=want_lse),
        grid=(n_batch, nb, n_cb),
        in_specs=in_specs,
        out_specs=[o_spec, o_spec] if want_lse else [o_spec],
        out_shape=[o_shape, o_shape] if want_lse else [o_shape],
        compiler_params=_cp("parallel", "arbitrary", "arbitrary"),
        name="banded_attn",
    )(*args)
    return [t.reshape(n_batch * seq, n_cb * qw) for t in outs]


def _kv_tail_kernel(k_ref, v_ref, o_ref):
    w = k_ref.shape[1]
    o_ref[:w, :] = k_ref[...].T
    o_ref[w:, :] = v_ref[...].T


def _kv_tail(p, n_batch, seq, c0, hk, keep):
    n_in = p.shape[1]
    w = hk * HEAD_DIM
    rows = math.gcd(keep, 512)
    assert c0 % w == 0 and rows % LANES == 0 and (seq - keep) % rows == 0
    r0 = (seq - keep) // rows
    p3 = p.reshape(n_batch, seq, n_in)
    t = pl.pallas_call(
        _kv_tail_kernel,
        grid=(n_batch, keep // rows),
        in_specs=[pl.BlockSpec((None, rows, w), lambda b, i: (b, r0 + i, c0 // w)),
                  pl.BlockSpec((None, rows, w), lambda b, i: (b, r0 + i, c0 // w + 1))],
        out_specs=pl.BlockSpec((None, 2 * w, rows), lambda b, i: (b, 0, i)),
        out_shape=jax.ShapeDtypeStruct((n_batch, 2 * w, keep), F32),
        compiler_params=_cp("parallel", "parallel"),
        name="kv_tail",
    )(p3, p3)
    return jnp.transpose(t.reshape(n_batch, 2, hk, HEAD_DIM, keep), (0, 4, 1, 2, 3))


def _win_decode_kernel(q_ref, kn_ref, vn_ref, c_ref, *rest, bb, hk, grp, dil, win, has_sink, want_lse):
    rest = list(rest)
    sink_ref = rest.pop(0) if has_sink else None
    o_ref = rest.pop(0)
    lse_ref = rest.pop(0) if want_lse else None
    nc_ref, pad_ref = rest
    s_len = q_ref.shape[1]
    lb = c_ref.shape[2]
    dh = HEAD_DIM
    kvw = hk * dh
    n_row = grp * s_len
    new0 = LANES - s_len
    s_c = lax.broadcasted_iota(jnp.int32, (n_row, lb), 0) % s_len
    dist_c = lb + s_c - lax.broadcasted_iota(jnp.int32, (n_row, lb), 1)
    valid_c = (dist_c <= win) & ((dist_c & (dil - 1)) == 0)
    s_n = lax.broadcasted_iota(jnp.int32, (n_row, LANES), 0) % s_len
    c_n = lax.broadcasted_iota(jnp.int32, (n_row, LANES), 1) - new0
    dist_n = s_n - c_n
    valid_n = (c_n >= 0) & (dist_n >= 0) & (dist_n <= win) & ((dist_n & (dil - 1)) == 0)
    is_new = lax.broadcasted_iota(jnp.int32, (dh, LANES), 1) >= new0
    pad_ref[:new0, :] = jnp.zeros((new0, 2 * kvw), F32)
    for b in range(bb):
        pad_ref[new0:, :kvw] = kn_ref[b]
        pad_ref[new0:, kvw:] = vn_ref[b]
        new_t = pad_ref[...].T
        for h in range(hk):
            k_rows = slice(h * dh, (h + 1) * dh)
            v_rows = slice(kvw + h * dh, kvw + (h + 1) * dh)
            k_t = c_ref[b, k_rows, :]
            v_t = c_ref[b, v_rows, :]
            kn_t = new_t[k_rows]
            vn_t = new_t[v_rows]
            q = jnp.concatenate([q_ref[b, :, (h * grp + g) * dh:(h * grp + g + 1) * dh] for g in range(grp)],
                                axis=0)
            sc = jnp.where(valid_c, _dot(q.astype(BF16), k_t.astype(BF16)), NEG)
            sn = jnp.where(valid_n, _dot(q, kn_t), NEG)
            m = jnp.maximum(jnp.max(sc, axis=-1, keepdims=True), jnp.max(sn, axis=-1, keepdims=True))
            if has_sink:
                sk = jnp.concatenate(
                    [jnp.broadcast_to(sink_ref[:, h * grp + g:h * grp + g + 1], (s_len, 1)) for g in range(grp)],
                    axis=0)
                m = jnp.maximum(m, sk)
            pc = jnp.exp(sc - m)
            pn = jnp.exp(sn - m)
            l = jnp.sum(pc, axis=-1, keepdims=True) + jnp.sum(pn, axis=-1, keepdims=True)
            if has_sink:
                l = l + jnp.exp(sk - m)
            o = (_dot_nt(pc.astype(BF16), v_t.astype(BF16)) + _dot_nt(pn, vn_t)) / l
            for g in range(grp):
                qs = slice((h * grp + g) * dh, (h * grp + g + 1) * dh)
                o_ref[b, :, qs] = o[g * s_len:(g + 1) * s_len]
                if want_lse:
                    lse_ref[b, :, qs] = jnp.broadcast_to((m + jnp.log(l))[g * s_len:(g + 1) * s_len], (s_len, dh))
            for rows, x_t, x_new in ((k_rows, k_t, kn_t), (v_rows, v_t, vn_t)):
                moved = pltpu.roll(x_t, lb - s_len, axis=1)
                nc_ref[b, rows, :] = moved
                nc_ref[b, rows, lb - LANES:] = jnp.where(is_new, x_new, moved[:, lb - LANES:])


def _win_decode(p, n_batch, s_len, buf, q_blk, k_blk, v_blk, hk, grp, dil, win, sinks=None, want_lse=False):
    n_in = p.shape[1]
    lb = buf.shape[1]
    assert lb == win and lb % LANES == 0 and dil & (dil - 1) == 0 and s_len <= LANES
    dh = HEAD_DIM
    qw, kvw = hk * grp * dh, hk * dh
    buf_t = jnp.transpose(buf, (0, 2, 3, 4, 1)).reshape(n_batch, 2 * kvw, lb)
    bb = max(1, min(math.gcd(n_batch, 8), (2 * 1024 * 1024) // (lb * 2 * kvw * 4)))
    p3 = p.reshape(n_batch, s_len, n_in)
    (qwid, qi), (kwid, ki), (vwid, vi) = q_blk, k_blk, v_blk
    assert qwid == qw and kwid == kvw and vwid == kvw
    q_spec = pl.BlockSpec((bb, s_len, qw), lambda i: (i, 0, 0))
    buf_spec = pl.BlockSpec((bb, 2 * kvw, lb), lambda i: (i, 0, 0))
    in_specs = [
        pl.BlockSpec((bb, s_len, qw), lambda i: (i, 0, qi)),
        pl.BlockSpec((bb, s_len, kvw), lambda i: (i, 0, ki)),
        pl.BlockSpec((bb, s_len, kvw), lambda i: (i, 0, vi)),
        buf_spec,
    ]
    args = [p3, p3, p3, buf_t]
    if sinks is not None:
        in_specs.append(pl.BlockSpec((1, hk * grp), lambda i: (0, 0)))
        args.append(sinks.reshape(1, -1).astype(F32))
    q_shape = jax.ShapeDtypeStruct((n_batch, s_len, qw), F32)
    outs = pl.pallas_call(
        functools.partial(_win_decode_kernel, bb=bb, hk=hk, grp=grp, dil=dil, win=win,
                          has_sink=sinks is not None, want_lse=want_lse),
        grid=(n_batch // bb,),
        in_specs=in_specs,
        out_specs=[q_spec] * (2 if want_lse else 1) + [buf_spec],
        out_shape=[q_shape] * (2 if want_lse else 1) + [jax.ShapeDtypeStruct(buf_t.shape, F32)],
        scratch_shapes=[pltpu.VMEM((LANES, 2 * kvw), F32)],
        compiler_params=_cp("parallel"),
        name="win_decode",
    )(*args)
    nbuf = jnp.transpose(outs[-1].reshape(n_batch, 2, hk, dh, lb), (0, 4, 1, 2, 3))
    return [t.reshape(n_batch * s_len, qw) for t in outs[:-1]] + [nbuf]


def _diff_lambda(lam_ref):
    lam = lam_ref[...]
    a = jnp.sum(lam[0:1] * lam[1:2], axis=-1, keepdims=True)
    b = jnp.sum(lam[2:3] * lam[3:4], axis=-1, keepdims=True)
    return jnp.exp(a) - jnp.exp(b) + LAMBDA_INIT


def _diff_head_out(a1, a2, lam, sub_g):
    o = a1 - lam * a2
    return o * lax.rsqrt(jnp.mean(o * o, axis=-1, keepdims=True) + EPS) * sub_g * (1.0 - LAMBDA_INIT)


def _diff_prompt_kernel(q_ref, kv_ref, lam_ref, sub_ref, o_ref, qb_ref, m_ref, l_ref, acc_ref, *, tq):
    qi = pl.program_id(1)
    dh = HEAD_DIM
    dv = 2 * dh
    kw = DIFF_KV_HEADS * dv
    rows2 = DIFF_GROUP * tq
    r_i = lax.broadcasted_iota(jnp.int32, (rows2, tq), 0) % tq
    c_i = lax.broadcasted_iota(jnp.int32, (rows2, tq), 1)
    diag = c_i <= r_i
    lam = _diff_lambda(lam_ref)
    for h in range(DIFF_KV_HEADS):
        for mp in range(2):
            qb_ref[h * 2 + mp] = jnp.concatenate(
                [q_ref[:, ((h * DIFF_GROUP + g) * 2 + mp) * dh:((h * DIFF_GROUP + g) * 2 + mp + 1) * dh]
                 for g in range(DIFF_GROUP)], axis=0).astype(BF16)
    m_ref[...] = jnp.full_like(m_ref, NEG)
    l_ref[...] = jnp.zeros_like(l_ref)
    acc_ref[...] = jnp.zeros_like(acc_ref)

    def step(kb, carry, masked):
        r0 = pl.multiple_of(kb * tq, tq)
        n_map = DIFF_KV_HEADS * 2
        old = [(m_ref[c], l_ref[c], acc_ref[c]) for c in range(n_map)]
        new = []
        for h in range(DIFF_KV_HEADS):
            v = kv_ref[pl.ds(r0, tq), kw + h * dv:kw + (h + 1) * dv].astype(BF16)
            for mp in range(2):
                m_old, l_old, acc_old = old[h * 2 + mp]
                k = kv_ref[pl.ds(r0, tq), h * dv + mp * dh:h * dv + (mp + 1) * dh].astype(BF16)
                s = _dot_nt(qb_ref[h * 2 + mp], k)
                if masked:
                    s = jnp.where(diag, s, NEG)
                parts = [s[:, i * LANES:(i + 1) * LANES] for i in range(tq // LANES)]
                mx = parts[0]
                for t in parts[1:]:
                    mx = jnp.maximum(mx, t)
                m_new = jnp.maximum(m_old, jnp.max(mx, axis=-1, keepdims=True))
                corr = jnp.exp(m_old - m_new)
                ps = [jnp.exp(t - m_new) for t in parts]
                l_new = l_old * corr
                for t in ps:
                    l_new = l_new + t
                p = jnp.concatenate(ps, axis=1).astype(BF16)
                new.append((m_new, l_new, acc_old * corr + _dot(p, v)))
        for c in range(n_map):
            m_ref[c], l_ref[c], acc_ref[c] = new[c]
        return carry

    lax.fori_loop(0, qi, functools.partial(step, masked=False), 0)
    step(qi, 0, True)
    for h in range(DIFF_KV_HEADS):
        res = [acc_ref[h * 2 + mp] / jnp.sum(l_ref[h * 2 + mp], axis=-1, keepdims=True) for mp in range(2)]
        for g in range(DIFF_GROUP):
            hq = h * DIFF_GROUP + g
            o_ref[:, hq * dv:(hq + 1) * dv] = _diff_head_out(res[0][g * tq:(g + 1) * tq], res[1][g * tq:(g + 1) * tq],
                                                             lam, sub_ref[...])


def _diff_prompt(p, n_batch, seq, lam_vec, subln_g):
    n_in = p.shape[1]
    nq = DIFF_KV_HEADS * DIFF_GROUP * 2 * HEAD_DIM
    tq = math.gcd(seq, 256)
    n_map = DIFF_KV_HEADS * 2
    assert 2 * HEAD_DIM == LANES and tq % LANES == 0
    p3 = p.reshape(n_batch, seq, n_in)
    o = pl.pallas_call(
        functools.partial(_diff_prompt_kernel, tq=tq),
        grid=(n_batch, seq // tq),
        in_specs=[
            pl.BlockSpec((None, tq, nq), lambda b, i: (b, i, 0)),
            pl.BlockSpec((None, seq, nq), lambda b, i: (b, 0, 1)),
            pl.BlockSpec((4, HEAD_DIM), lambda b, i: (0, 0)),
            pl.BlockSpec((1, 2 * HEAD_DIM), lambda b, i: (0, 0)),
        ],
        out_specs=pl.BlockSpec((None, tq, nq), lambda b, i: (b, i, 0)),
        out_shape=jax.ShapeDtypeStruct((n_batch, seq, nq), F32),
        scratch_shapes=[pltpu.VMEM((n_map, DIFF_GROUP * tq, HEAD_DIM), BF16)]
        + [pltpu.VMEM((n_map, DIFF_GROUP * tq, LANES), F32)] * 3,
        compiler_params=_cp("parallel", "arbitrary"),
        name="diff_attn",
    )(p3, p3, lam_vec.astype(F32), subln_g.reshape(1, -1).astype(F32))
    return o.reshape(n_batch * seq, nq)


def _diff_sample_kernel(pt_ref, q_ref, kv_ref, lam_ref, sub_ref, *rest, pps):
    pages = rest[:pps]
    o_ref, qb_ref, m_ref, l_ref, acc_ref, nk_ref = rest[pps:]
    step = pl.program_id(1)
    n_step = pl.num_programs(1)
    s_len = q_ref.shape[0]
    dh = HEAD_DIM
    dv = 2 * dh
    kw = DIFF_KV_HEADS * dv
    hr = DIFF_GROUP * 2 * s_len
    stride = 2 * DIFF_KV_HEADS

    @pl.when(step == 0)
    def _():
        qb_ref[...] = jnp.zeros_like(qb_ref)
        for idx in range(DIFF_KV_HEADS * DIFF_GROUP * 2):
            mp = idx % 2
            qb_ref[idx * s_len:(idx + 1) * s_len, mp * dh:(mp + 1) * dh] = q_ref[:, idx * dh:(idx + 1) * dh]
        m_ref[...] = jnp.full_like(m_ref, NEG)
        l_ref[...] = jnp.zeros_like(l_ref)
        acc_ref[...] = jnp.zeros_like(acc_ref)

    def update(parts_h, v_h):
        heads = range(DIFF_KV_HEADS)
        old = [(m_ref[h * hr:(h + 1) * hr], l_ref[h * hr:(h + 1) * hr], acc_ref[h * hr:(h + 1) * hr]) for h in heads]
        new = []
        for h in heads:
            m_old, l_old, acc_old = old[h]
            parts = parts_h[h]
            mx = parts[0]
            for t in parts[1:]:
                mx = jnp.maximum(mx, t)
            m_new = jnp.maximum(m_old, jnp.max(mx, axis=-1, keepdims=True))
            corr = jnp.exp(m_old - m_new)
            ps = [jnp.exp(t - m_new) for t in parts]
            l_new = l_old * corr
            for t in ps:
                l_new = l_new + t
            p = ps[0] if len(ps) == 1 else jnp.concatenate(ps, axis=1)
            new.append((m_new, l_new, acc_old * corr + _dot(p.astype(BF16), v_h[h])))
        for h in heads:
            m_ref[h * hr:(h + 1) * hr], l_ref[h * hr:(h + 1) * hr], acc_ref[h * hr:(h + 1) * hr] = new[h]

    qhs = [qb_ref[h * hr:(h + 1) * hr, :].astype(BF16) for h in range(DIFF_KV_HEADS)]
    update([[_dot_nt(qhs[h], pg[pl.ds(h, PAGE_SIZE, stride=stride), :].astype(BF16)) for pg in pages]
            for h in range(DIFF_KV_HEADS)],
           [jnp.concatenate([pg[pl.ds(DIFF_KV_HEADS + h, PAGE_SIZE, stride=stride), :] for pg in pages],
                            axis=0).astype(BF16) for h in range(DIFF_KV_HEADS)])

    @pl.when(step == n_step - 1)
    def _():
        nk_ref[...] = jnp.zeros_like(nk_ref)
        nk_ref[:s_len, :] = kv_ref[...]
        r_s = lax.broadcasted_iota(jnp.int32, (hr, PAGE_SIZE), 0) % s_len
        c_s = lax.broadcasted_iota(jnp.int32, (hr, PAGE_SIZE), 1)
        update([[jnp.where(c_s <= r_s, _dot_nt(qhs[h], nk_ref[:, h * dv:(h + 1) * dv].astype(BF16)), NEG)]
                for h in range(DIFF_KV_HEADS)],
               [nk_ref[:, kw + h * dv:kw + (h + 1) * dv].astype(BF16) for h in range(DIFF_KV_HEADS)])
        lam = _diff_lambda(lam_ref)
        res = acc_ref[...] / jnp.sum(l_ref[...], axis=-1, keepdims=True)
        for hq in range(DIFF_KV_HEADS * DIFF_GROUP):
            a1 = res[(hq * 2) * s_len:(hq * 2 + 1) * s_len]
            a2 = res[(hq * 2 + 1) * s_len:(hq * 2 + 2) * s_len]
            o_ref[:, hq * dv:(hq + 1) * dv] = _diff_head_out(a1, a2, lam, sub_ref[...])


def _diff_sample(p, n_batch, s_len, cache, page_table, lam_vec, subln_g):
    n_in = p.shape[1]
    nq = DIFF_KV_HEADS * DIFF_GROUP * 2 * HEAD_DIM
    n_pages = page_table.shape[1]
    pps = math.gcd(n_pages, 16)
    n_rows = DIFF_KV_HEADS * DIFF_GROUP * 2 * s_len
    p3 = p.reshape(n_batch, s_len, n_in)
    kvw = 2 * DIFF_KV_HEADS * 2 * HEAD_DIM
    assert 2 * HEAD_DIM == LANES and PAGE_SIZE == LANES
    page_rows = PAGE_SIZE * 2 * DIFF_KV_HEADS
    cache3 = cache.reshape(cache.shape[0] * page_rows, LANES)

    def page_spec(jj):
        return pl.BlockSpec((page_rows, LANES), lambda b, s, pt: (pt[b * n_pages + s * pps + jj], 0))

    grid_spec = pltpu.PrefetchScalarGridSpec(
        num_scalar_prefetch=1,
        grid=(n_batch, n_pages // pps),
        in_specs=[
            pl.BlockSpec((None, s_len, nq), lambda b, s, pt: (b, 0, 0)),
            pl.BlockSpec((None, s_len, nq), lambda b, s, pt: (b, 0, 1)),
            pl.BlockSpec((4, HEAD_DIM), lambda b, s, pt: (0, 0)),
            pl.BlockSpec((1, 2 * HEAD_DIM), lambda b, s, pt: (0, 0)),
        ] + [page_spec(jj) for jj in range(pps)],
        out_specs=pl.BlockSpec((None, s_len, nq), lambda b, s, pt: (b, 0, 0)),
        scratch_shapes=[pltpu.VMEM((n_rows, LANES), F32)] * 4 + [pltpu.VMEM((PAGE_SIZE, kvw), F32)],
    )
    o = pl.pallas_call(
        functools.partial(_diff_sample_kernel, pps=pps),
        grid_spec=grid_spec,
        out_shape=jax.ShapeDtypeStruct((n_batch, s_len, nq), F32),
        compiler_params=_cp("parallel", "arbitrary"),
        name="diff_decode",
    )(page_table.reshape(-1), p3, p3, lam_vec.astype(F32), subln_g.reshape(1, -1).astype(F32),
      *([cache3] * pps))
    return o.reshape(n_batch * s_len, nq)


def kernel(x_prompt, x_sample, state_gla, cache_swa, cache_dil1, cache_dil2, cache_dil3, cache_diff, page_table, c_prompt, c_sample, norm1_g, norm2_g, ada_w, ada_b, gla_w_in, gla_w_gate2, gla_b_gate, gla_onorm_g, gla_w_out, swa_w_in, swa_q_norm, swa_k_norm, swa_sinks, swa_w_out, dil_w_in, dil_q_norm, dil_k_norm, dil_w_out, diff_w_in, diff_q_norm, diff_k_norm, diff_lambda, diff_subln_g, diff_w_out, ffn_w_gu, ffn_w_down, moe_router, moe_w_gu, moe_w_down):
    bp, seq, d = x_prompt.shape
    db, s_len, _ = x_sample.shape
    depth = ada_w.shape[0]
    dh = HEAD_DIM
    qscale = dh ** -0.5

    mods = _ada(jnp.concatenate([c_prompt, c_sample], axis=0), ada_w, ada_b)
    xp = x_prompt.reshape(bp * seq, d)
    xs = x_sample.reshape(db * s_len, d)
    out = {}

    for i in range(depth):
        gp = _Group(bp, seq, mods[i, :bp])
        gs = _Group(db, s_len, mods[i, bp:])
        kind = i % 4
        if kind == 0:
            n_in = gla_w_in.shape[1]
            n_pad = -(-n_in // (5 * LANES)) * (5 * LANES)
            w_in = jnp.pad(gla_w_in, ((0, 0), (0, n_pad - n_in)))
            lr_w = n_pad - (n_in - GLA_RANK)
            wg2 = jnp.pad(gla_w_gate2, ((0, lr_w - GLA_RANK), (0, 0)))
            pp = _inproj(gp, xp, norm1_g[i], 0, w_in, n_pad // 5)
            ps = _inproj(gs, xs, norm1_g[i], 0, w_in, n_pad // 5)
            op, out["gla_p"] = _gla(pp, bp, seq, wg2, gla_b_gate, gla_onorm_g, None)
            os_, out["gla_s"] = _gla(ps, db, s_len, wg2, gla_b_gate, gla_onorm_g, state_gla)
            xp = _outproj(gp, xp, op, gla_w_out, 2)
            xs = _outproj(gs, xs, os_, gla_w_out, 2)
        elif kind == 1:
            hk = SWA_KV_HEADS
            nq = swa_w_out.shape[0]
            grp = nq // dh // hk
            kvw = hk * dh
            norm = _norm_rows([(nq, swa_q_norm, qscale), (kvw, swa_k_norm, 1.0), (kvw, None, 1.0)])
            pp = _inproj(gp, xp, norm1_g[i], 0, swa_w_in, kvw * 2, norm)
            ps = _inproj(gs, xs, norm1_g[i], 0, swa_w_in, kvw * 2, norm)
            (op,) = _banded(pp, bp, seq, 1, (nq, 0), (kvw, nq // kvw), (kvw, nq // kvw + 1),
                            hk, grp, SWA_WINDOW, sinks=swa_sinks)
            keep = min(SWA_WINDOW, seq)
            out["swa_p"] = _kv_tail(pp, bp, seq, nq, hk, keep)
            os_, out["swa_s"] = _win_decode(ps, db, s_len, cache_swa, (nq, 0), (kvw, nq // kvw),
                                            (kvw, nq // kvw + 1), hk, grp, 1, SWA_WINDOW, sinks=swa_sinks)
            xp = _outproj(gp, xp, op, swa_w_out, 2)
            xs = _outproj(gs, xs, os_, swa_w_out, 2)
        elif kind == 2:
            w = DIL_HEADS * dh
            norm = _norm_rows([(w, dil_q_norm, qscale), (w, dil_k_norm, 1.0), (w, None, 1.0)] * len(DIL_GROUPS))
            pp = _inproj(gp, xp, norm1_g[i], 0, dil_w_in, w, norm)
            ps = _inproj(gs, xs, norm1_g[i], 0, dil_w_in, w, norm)
            outs_p, lses_p, outs_s, lses_s = [], [], [], []
            caches = (cache_dil1, cache_dil2, cache_dil3)
            for gi, (win, dil) in enumerate(DIL_GROUPS):
                o, lse = _banded(pp, bp, seq, dil, (w, 3 * gi), (w, 3 * gi + 1), (w, 3 * gi + 2),
                                 DIL_HEADS, 1, win // dil, want_lse=True)
                outs_p.append(o)
                lses_p.append(lse)
                keep = min(win, seq)
                out["dil%d_p" % gi] = _kv_tail(pp, bp, seq, (3 * gi + 1) * w, DIL_HEADS, keep)
                o, lse, out["dil%d_s" % gi] = _win_decode(ps, db, s_len, caches[gi], (w, 3 * gi), (w, 3 * gi + 1),
                                                          (w, 3 * gi + 2), DIL_HEADS, 1, dil, win, want_lse=True)
                outs_s.append(o)
                lses_s.append(lse)
            xp = _outproj_mix(gp, xp, outs_p, lses_p, dil_w_out, 2)
            xs = _outproj_mix(gs, xs, outs_s, lses_s, dil_w_out, 2)
        else:
            nq = DIFF_KV_HEADS * DIFF_GROUP * 2 * dh
            nk = DIFF_KV_HEADS * 2 * dh
            norm = _norm_rows([(nq, diff_q_norm, qscale), (nk, diff_k_norm, 1.0), (nk, None, 1.0)])
            pp = _inproj(gp, xp, norm1_g[i], 0, diff_w_in, nk, norm)
            ps = _inproj(gs, xs, norm1_g[i], 0, diff_w_in, nk, norm)
            op = _diff_prompt(pp, bp, seq, diff_lambda, diff_subln_g)
            os_ = _diff_sample(ps, db, s_len, cache_diff, page_table, diff_lambda, diff_subln_g)
            out["diff_p"] = pp[:, nq:].reshape(bp, seq, 2, DIFF_KV_HEADS, 2 * dh)
            out["diff_s"] = ps[:, nq:].reshape(db, s_len, 2, DIFF_KV_HEADS, 2 * dh)
            xp = _outproj(gp, xp, op, diff_w_out, 2)
            xs = _outproj(gs, xs, os_, diff_w_out, 2)
        j = i // 2
        if i % 2 == 0:
            w_gu, w_down = ffn_w_gu[j].astype(BF16), ffn_w_down[j].astype(BF16)
            xp = _ffn(gp, xp, norm2_g[i], w_gu, w_down, 1408)
            xs = _ffn(gs, xs, norm2_g[i], w_gu, w_down, 1408)
        else:
            xp = _moe(gp, xp, norm2_g[i], moe_router[j], moe_w_gu[j], moe_w_down[j], 512)
            xs = _moe(gs, xs, norm2_g[i], moe_router[j], moe_w_gu[j], moe_w_down[j], 512)

    return (xp.reshape(bp, seq, d), xs.reshape(db, s_len, d), out["gla_p"], out["gla_s"],
            out["swa_p"], out["swa_s"], out["dil0_p"], out["dil0_s"], out["dil1_p"], out["dil1_s"],
            out["dil2_p"], out["dil2_s"], out["diff_p"], out["diff_s"])
```

```python
import functools
import math

import jax
import jax.numpy as jnp
from jax import lax
from jax.experimental import pallas as pl
from jax.experimental.pallas import tpu as pltpu

F32 = jnp.float32
BF16 = jnp.bfloat16

EPS = 1e-6
NEG = -1e30
BLK = 128
HEAD_DIM = 64

GLA_HEADS = 4
GLA_RANK = 16
GLA_NORMALIZER = 16.0
GLA_CHUNK = 64

SWA_KV_HEADS = 4
SWA_WINDOW = 128
DIL_GROUPS = ((128, 1), (512, 4), (2048, 16))
DIL_HEADS = 8
DIFF_KV_HEADS = 4
DIFF_GROUP = 2
LAMBDA_INIT = 0.8 - 0.6 * math.exp(-0.3 * 3)
N_EXPERTS = 8
PAGE_SIZE = 128

LANES = 128
VMEM_LIMIT = 56 * 1024 * 1024


def _cp(*sem):
    return pltpu.CompilerParams(dimension_semantics=sem, vmem_limit_bytes=VMEM_LIMIT)


def _dot(a, b):
    return jnp.dot(a, b, preferred_element_type=F32)


def _dot_nt(a, b):
    return lax.dot_general(a, b, (((1,), (1,)), ((), ())), preferred_element_type=F32)


def _dot_tn(a, b):
    return lax.dot_general(a, b, (((0,), (0,)), ((), ())), preferred_element_type=F32)


def _silu(x):
    return x / (1.0 + jnp.exp(-x))


def _adaln(x, g, shift, scale):
    y = x * lax.rsqrt(jnp.mean(x * x, axis=-1, keepdims=True) + EPS)
    return y * g * (1.0 + scale) + shift


class _Group:
    def __init__(self, n_batch, rows_per_batch, mods):
        self.m = n_batch * rows_per_batch
        self.rows_per_batch = rows_per_batch
        self.prompt = rows_per_batch >= 256
        d6 = mods.shape[-1]
        if self.prompt:
            self.mods = mods.reshape(n_batch, 1, d6)
        else:
            self.mods = jnp.repeat(mods, rows_per_batch, axis=0)
        self.d = d6 // 6
        self.tm = self.tile(1024)

    def tile(self, tm_max):
        return math.gcd(self.rows_per_batch if self.prompt else self.m, tm_max)

    def mod_spec(self, k, tm=None):
        d = self.d
        tm = tm or self.tm
        if self.prompt:
            tpb = self.rows_per_batch // tm
            return pl.BlockSpec((None, 1, d), lambda i, *_: (i // tpb, 0, k))
        return pl.BlockSpec((tm, d), lambda i, *_: (i, k))


def _ada_kernel(c_ref, w_ref, b_ref, o_ref):
    a = _silu(c_ref[...]).astype(BF16)
    o_ref[...] = _dot(a, w_ref[...].astype(BF16)) + b_ref[...]


def _ada(c_all, ada_w, ada_b):
    depth, d, d6 = ada_w.shape
    nb = c_all.shape[0]
    tn = 1024
    return pl.pallas_call(
        _ada_kernel,
        grid=(depth, d6 // tn),
        in_specs=[
            pl.BlockSpec((nb, d), lambda l, j: (0, 0)),
            pl.BlockSpec((None, d, tn), lambda l, j: (l, 0, j)),
            pl.BlockSpec((None, 1, tn), lambda l, j: (l, 0, j)),
        ],
        out_specs=pl.BlockSpec((None, nb, tn), lambda l, j: (l, 0, j)),
        out_shape=jax.ShapeDtypeStruct((depth, nb, d6), F32),
        compiler_params=_cp("parallel", "parallel"),
        name="ada_mod",
    )(c_all, ada_w, ada_b.reshape(depth, 1, d6))


def _inproj_kernel(x_ref, g_ref, sh_ref, sc_ref, w_ref, *rest, norm):
    if norm:
        flag_ref, gain_ref, o_ref, h_ref = rest
    else:
        o_ref, h_ref = rest
    j = pl.program_id(1)

    @pl.when(j == 0)
    def _():
        h_ref[...] = _adaln(x_ref[...], g_ref[...], sh_ref[...], sc_ref[...]).astype(BF16)

    acc = _dot(h_ref[...], w_ref[...].astype(BF16))
    if not norm:
        o_ref[...] = acc
        return
    tn = acc.shape[1]
    lo = lax.broadcasted_iota(jnp.int32, (1, LANES), 1) < HEAD_DIM
    for c in range(tn // LANES):
        sl = slice(c * LANES, (c + 1) * LANES)
        a = acc[:, sl]
        sq = a * a
        s_lo = jnp.sum(jnp.where(lo, sq, 0.0), axis=-1, keepdims=True)
        s_hi = jnp.sum(jnp.where(lo, 0.0, sq), axis=-1, keepdims=True)
        r = lax.rsqrt(jnp.where(lo, s_lo, s_hi) * (1.0 / HEAD_DIM) + EPS)
        o_ref[:, sl] = a * jnp.where(flag_ref[:, sl] > 0.0, r, 1.0) * gain_ref[:, sl]


def _inproj(grp, x, norm_g, k_shift, w, tn, norm=None):
    m, d = x.shape
    n = w.shape[1]
    tm = grp.tm
    in_specs = [
        pl.BlockSpec((tm, d), lambda i, j: (i, 0)),
        pl.BlockSpec((1, d), lambda i, j: (0, 0)),
        grp.mod_spec(k_shift),
        grp.mod_spec(k_shift + 1),
        pl.BlockSpec((d, tn), lambda i, j: (0, j)),
    ]
    args = [x, norm_g.reshape(1, d), grp.mods, grp.mods, w]
    if norm is not None:
        in_specs += [pl.BlockSpec((1, tn), lambda i, j: (0, j))] * 2
        args += list(norm)
    return pl.pallas_call(
        functools.partial(_inproj_kernel, norm=norm is not None),
        grid=(m // tm, n // tn),
        in_specs=in_specs,
        out_specs=pl.BlockSpec((tm, tn), lambda i, j: (i, j)),
        out_shape=jax.ShapeDtypeStruct((m, n), F32),
        scratch_shapes=[pltpu.VMEM((tm, d), BF16)],
        compiler_params=_cp("parallel", "arbitrary"),
        name="adaln_inproj",
    )(*args)


def _norm_rows(pieces):
    flags, gains = [], []
    for width, g, scale in pieces:
        if g is None:
            flags.append(jnp.zeros((width,), F32))
            gains.append(jnp.ones((width,), F32))
        else:
            flags.append(jnp.ones((width,), F32))
            gains.append(jnp.tile(g.astype(F32) * scale, width // HEAD_DIM))
    return jnp.concatenate(flags)[None, :], jnp.concatenate(gains)[None, :]


def _outproj_kernel(x_ref, o_ref, gt_ref, w_ref, out_ref):
    y = _dot(o_ref[...].astype(BF16), w_ref[...].astype(BF16))
    out_ref[...] = x_ref[...] + gt_ref[...] * y


def _outproj(grp, x, o, w, k_gate):
    m, d = x.shape
    kdim = w.shape[0]
    tm = grp.tm
    return pl.pallas_call(
        _outproj_kernel,
        grid=(m // tm,),
        in_specs=[
            pl.BlockSpec((tm, d), lambda i: (i, 0)),
            pl.BlockSpec((tm, kdim), lambda i: (i, 0)),
            grp.mod_spec(k_gate),
            pl.BlockSpec((kdim, d), lambda i: (0, 0)),
        ],
        out_specs=pl.BlockSpec((tm, d), lambda i: (i, 0)),
        out_shape=jax.ShapeDtypeStruct((m, d), F32),
        compiler_params=_cp("parallel"),
        name="outproj",
    )(x, o, grp.mods, w)


def _outproj_mix_kernel(x_ref, o0_ref, o1_ref, o2_ref, l0_ref, l1_ref, l2_ref, gt_ref, w_ref, out_ref):
    l0, l1, l2 = l0_ref[...], l1_ref[...], l2_ref[...]
    mx = jnp.maximum(jnp.maximum(l0, l1), l2)
    e0, e1, e2 = jnp.exp(l0 - mx), jnp.exp(l1 - mx), jnp.exp(l2 - mx)
    o = (e0 * o0_ref[...] + e1 * o1_ref[...] + e2 * o2_ref[...]) / (e0 + e1 + e2)
    y = _dot(o.astype(BF16), w_ref[...].astype(BF16))
    out_ref[...] = x_ref[...] + gt_ref[...] * y


def _outproj_mix(grp, x, outs, lses, w, k_gate):
    m, d = x.shape
    kdim = w.shape[0]
    tm = grp.tm
    row = pl.BlockSpec((tm, kdim), lambda i: (i, 0))
    return pl.pallas_call(
        _outproj_mix_kernel,
        grid=(m // tm,),
        in_specs=[pl.BlockSpec((tm, d), lambda i: (i, 0))] + [row] * 6
        + [grp.mod_spec(k_gate), pl.BlockSpec((kdim, d), lambda i: (0, 0))],
        out_specs=pl.BlockSpec((tm, d), lambda i: (i, 0)),
        out_shape=jax.ShapeDtypeStruct((m, d), F32),
        compiler_params=_cp("parallel"),
        name="outproj_mix",
    )(x, *outs, *lses, grp.mods, w)


def _ffn_kernel(x_ref, g_ref, sh_ref, sc_ref, gt_ref, wg_ref, wu_ref, wd_ref, o_ref, h_ref):
    f = pl.program_id(1)
    nf = pl.num_programs(1)

    @pl.when(f == 0)
    def _():
        h_ref[...] = _adaln(x_ref[...], g_ref[...], sh_ref[...], sc_ref[...]).astype(BF16)
        o_ref[...] = jnp.zeros_like(o_ref)

    h = h_ref[...]
    gate = _dot(h, wg_ref[...])
    up = _dot(h, wu_ref[...])
    a = (_silu(gate) * up).astype(BF16)
    o_ref[...] += _dot(a, wd_ref[...])

    @pl.when(f == nf - 1)
    def _():
        o_ref[...] = x_ref[...] + gt_ref[...] * o_ref[...]


def _ffn(grp, x, norm_g, w_gu, w_down, tf):
    m, d = x.shape
    dff = w_down.shape[0]
    tm = grp.tm
    nf = dff // tf
    return pl.pallas_call(
        _ffn_kernel,
        grid=(m // tm, nf),
        in_specs=[
            pl.BlockSpec((tm, d), lambda i, f: (i, 0)),
            pl.BlockSpec((1, d), lambda i, f: (0, 0)),
            grp.mod_spec(3), grp.mod_spec(4), grp.mod_spec(5),
            pl.BlockSpec((d, tf), lambda i, f: (0, f)),
            pl.BlockSpec((d, tf), lambda i, f: (0, f + nf)),
            pl.BlockSpec((tf, d), lambda i, f: (f, 0)),
        ],
        out_specs=pl.BlockSpec((tm, d), lambda i, f: (i, 0)),
        out_shape=jax.ShapeDtypeStruct((m, d), F32),
        scratch_shapes=[pltpu.VMEM((tm, d), BF16)],
        compiler_params=_cp("parallel", "arbitrary"),
        name="ffn_swiglu",
    )(x, norm_g.reshape(1, d), grp.mods, grp.mods, grp.mods, w_gu, w_gu, w_down)


def _route_top2(logits):
    lane = lax.broadcasted_iota(jnp.int32, logits.shape, 1)
    lg = jnp.where(lane < N_EXPERTS, logits, -jnp.inf)
    m1 = jnp.max(lg, axis=-1, keepdims=True)
    i1 = jnp.min(jnp.where(lg == m1, lane, LANES), axis=-1, keepdims=True)
    lg2 = jnp.where(lane == i1, -jnp.inf, lg)
    m2 = jnp.max(lg2, axis=-1, keepdims=True)
    i2 = jnp.min(jnp.where(lg2 == m2, lane, LANES), axis=-1, keepdims=True)
    e2 = jnp.exp(m2 - m1)
    g1 = 1.0 / (1.0 + e2)
    g2 = e2 / (1.0 + e2)
    comb = jnp.where(lane == i1, g1, 0.0) + jnp.where(lane == i2, g2, 0.0)
    return comb, (lane == i1) | (lane == i2)


def _moe_kernel(x_ref, g_ref, sh_ref, sc_ref, gt_ref, r_ref, wg_ref, wu_ref, wd_ref, o_ref,
                h_ref, comb_ref, key_ref, keyt_ref, cnt_ref, xs_ref, acc_ref, n_ref, *, sub):
    e = pl.program_id(1)
    f = pl.program_id(2)
    nf = pl.num_programs(2)
    tm = h_ref.shape[0]
    lane = lax.broadcasted_iota(jnp.int32, (tm, LANES), 1)

    @pl.when((e == 0) & (f == 0))
    def _():
        x = x_ref[...]
        h = _adaln(x, g_ref[...], sh_ref[...], sc_ref[...]).astype(BF16)
        h_ref[...] = h
        comb, sel = _route_top2(_dot(h, r_ref[...].astype(BF16)))
        comb_ref[...] = comb
        tb = math.gcd(tm, 256)
        tri = jnp.where(lax.broadcasted_iota(jnp.int32, (tb, tb), 0) >= lax.broadcasted_iota(jnp.int32, (tb, tb), 1),
                        1.0, 0.0).astype(BF16)
        carry = jnp.zeros((1, LANES), F32)
        for blk in range(tm // tb):
            rows = slice(blk * tb, (blk + 1) * tb)
            sb = jnp.where(sel[rows], 1.0, 0.0)
            incl = _dot(tri, sb.astype(BF16)) + carry
            key_ref[rows, :] = jnp.where(sb > 0.0, incl - 1.0, -1.0).astype(jnp.int32)
            carry = carry + jnp.sum(sb, axis=0, keepdims=True)
        cnt_ref[...] = carry
        keyt_ref[...] = key_ref[...].T
        o_ref[...] = x

    def onehot(r0, nr):
        keyrow = keyt_ref[pl.ds(e, 1), :]
        r = lax.broadcasted_iota(jnp.int32, (nr, tm), 0) + r0
        return jnp.where(keyrow == r, 1.0, 0.0).astype(BF16)

    def for_blocks(fn):
        def body(s, carry):
            fn(pl.multiple_of(s * sub, sub), sub)
            return carry
        lax.fori_loop(0, (n_e + sub - 1) // sub, body, 0)

    @pl.when(f == 0)
    def _():
        cnt = jnp.sum(jnp.where(lane[:1] == e, cnt_ref[...], 0.0), axis=-1, keepdims=True)
        n_ref[0] = cnt.astype(jnp.int32)[0, 0]

    n_e = n_ref[0]

    @pl.when(f == 0)
    def _():
        def pack(r0, nr):
            xs_ref[pl.ds(r0, nr), :] = _dot(onehot(r0, nr), h_ref[...]).astype(BF16)
            acc_ref[pl.ds(r0, nr), :] = jnp.zeros((nr, acc_ref.shape[1]), F32)
        for_blocks(pack)

    def expert(r0, nr):
        xb = xs_ref[pl.ds(r0, nr), :]
        a = (_silu(_dot(xb, wg_ref[...].astype(BF16))) * _dot(xb, wu_ref[...].astype(BF16))).astype(BF16)
        acc_ref[pl.ds(r0, nr), :] += _dot(a, wd_ref[...].astype(BF16))
    for_blocks(expert)

    @pl.when(f == nf - 1)
    def _():
        ce = jnp.sum(jnp.where(lane == e, comb_ref[...], 0.0), axis=-1, keepdims=True)
        c_hi = ce.astype(BF16).astype(F32)
        c_mid = (ce - c_hi).astype(BF16).astype(F32)
        c_lo = ce - c_hi - c_mid
        cw = jnp.where(lane == 0, c_hi, jnp.where(lane == 1, c_mid, jnp.where(lane == 2, c_lo, 0.0))).astype(BF16)
        kcol = jnp.sum(jnp.where(lane == e, key_ref[...].astype(F32), 0.0), axis=-1,
                       keepdims=True).astype(jnp.int32)

        def unpack(r0, nr):
            g3 = _dot(onehot(r0, nr), cw)
            gate = g3[:, 0:1] + g3[:, 1:2] + g3[:, 2:3]
            y = (acc_ref[pl.ds(r0, nr), :] * gate).astype(BF16)
            c = lax.broadcasted_iota(jnp.int32, (tm, nr), 1) + r0
            scatter = jnp.where(kcol == c, 1.0, 0.0).astype(BF16)
            o_ref[...] += gt_ref[...] * _dot(scatter, y)
        for_blocks(unpack)


def _moe(grp, x, norm_g, router, w_gu, w_down, layer, tf):
    m, d = x.shape
    _, ne, dff, _ = w_down.shape
    tm = grp.tile(2048)
    sub = math.gcd(tm, 256)
    nf = dff // tf
    router_p = jnp.pad(router, ((0, 0), (0, LANES - ne)))
    resident = dict(pipeline_mode=pl.Buffered(1))
    return pl.pallas_call(
        functools.partial(_moe_kernel, sub=sub),
        grid=(m // tm, ne, nf),
        in_specs=[
            pl.BlockSpec((tm, d), lambda i, e, f: (i, 0), **resident),
            pl.BlockSpec((1, d), lambda i, e, f: (0, 0)),
            grp.mod_spec(3, tm), grp.mod_spec(4, tm), grp.mod_spec(5, tm),
            pl.BlockSpec((d, LANES), lambda i, e, f: (0, 0)),
            pl.BlockSpec((None, None, d, tf), lambda i, e, f: (layer, e, 0, f)),
            pl.BlockSpec((None, None, d, tf), lambda i, e, f: (layer, e, 0, f + nf)),
            pl.BlockSpec((None, None, tf, d), lambda i, e, f: (layer, e, f, 0)),
        ],
        out_specs=pl.BlockSpec((tm, d), lambda i, e, f: (i, 0), **resident),
        out_shape=jax.ShapeDtypeStruct((m, d), F32),
        scratch_shapes=[
            pltpu.VMEM((tm, d), BF16), pltpu.VMEM((tm, LANES), F32), pltpu.VMEM((tm, LANES), jnp.int32),
            pltpu.VMEM((LANES, tm), jnp.int32), pltpu.VMEM((1, LANES), F32),
            pltpu.VMEM((tm, d), BF16), pltpu.VMEM((tm, d), F32),
            pltpu.SMEM((1,), jnp.int32),
        ],
        compiler_params=_cp("parallel", "arbitrary", "arbitrary"),
        name="moe",
    )(x, norm_g.reshape(1, d), grp.mods, grp.mods, grp.mods, router_p, w_gu, w_gu, w_down)


def _gla_kernel(p_ref, wg2_ref, bg_ref, on_ref, *rest, chunk, n_chunk, n_valid, has_s0):
    rest = list(rest)
    s0_ref = rest.pop(0) if has_s0 else None
    o_ref, sfin_ref, s_ref = rest[:3]
    if n_valid < chunk:
        src = rest[3]
        src[...] = jnp.zeros_like(src)
        src[:n_valid, :] = p_ref[...]
    else:
        src = p_ref
    li = pl.program_id(1)
    nl = pl.num_programs(1)
    nq = GLA_HEADS * 128
    nv = GLA_HEADS * 256
    dk, dv = 128, 256

    @pl.when(li == 0)
    def _():
        if has_s0:
            s_ref[...] = s0_ref[...]
        else:
            s_ref[...] = jnp.zeros_like(s_ref)

    c = chunk
    row = lax.broadcasted_iota(jnp.int32, (c, c), 0)
    col = lax.broadcasted_iota(jnp.int32, (c, c), 1)
    causal = row >= col
    tri = jnp.where(causal, 1.0, 0.0).astype(BF16)
    ones = jnp.ones((c, dv), BF16)
    live = None
    if n_valid < c:
        live = lax.broadcasted_iota(jnp.int32, (c, 1), 0) < n_valid
    for ci in range(n_chunk):
        rows = slice(ci * c, (ci + 1) * c)
        a_lr = src[rows, 2 * nq + 2 * nv:]
        z = _dot(a_lr.astype(BF16), wg2_ref[...].astype(BF16)) + bg_ref[...]
        la = (jnp.minimum(z, 0.0) - jnp.log1p(jnp.exp(-jnp.abs(z)))) * (1.0 / GLA_NORMALIZER)
        if live is not None:
            la = jnp.where(live, la, 0.0)
        hi = la.astype(BF16)
        lo = (la - hi.astype(F32)).astype(BF16)
        b = _dot(tri, hi) + _dot(tri, lo)
        for h in range(GLA_HEADS):
            ks = slice(h * dk, (h + 1) * dk)
            vs = slice(h * dv, (h + 1) * dv)
            bh = b[:, ks]
            bl = bh[c - 1:c, :]
            qh = src[rows, h * dk:(h + 1) * dk] * (dk ** -0.5)
            kh = src[rows, nq + h * dk:nq + (h + 1) * dk]
            vh = src[rows, 2 * nq + h * dv:2 * nq + (h + 1) * dv]
            gh = src[rows, 2 * nq + nv + h * dv:2 * nq + nv + (h + 1) * dv]
            vb = vh.astype(BF16)
            qd = (qh * jnp.exp(bh)).astype(BF16)
            ki = (kh * jnp.exp(-bh)).astype(BF16)
            kd = (kh * jnp.exp(bl - bh)).astype(BF16)
            att = jnp.where(causal, _dot_nt(qd, ki), 0.0).astype(BF16)
            s = s_ref[h]
            o = _dot(att, vb) + _dot(qd, s.astype(BF16))
            dl = _dot_tn(hi[:, ks], ones) + _dot_tn(lo[:, ks], ones)
            s_ref[h] = jnp.exp(dl) * s + _dot_tn(kd, vb)
            on = o * lax.rsqrt(jnp.mean(o * o, axis=-1, keepdims=True) + EPS) * on_ref[...]
            res = on * _silu(gh)
            if n_valid < c:
                o_ref[:, vs] = res[:n_valid]
            else:
                o_ref[rows, vs] = res

    @pl.when(li == nl - 1)
    def _():
        sfin_ref[...] = s_ref[...]


def _gla(p, n_batch, seq, w_gate2_p, b_gate, onorm_g, s0):
    n_in = p.shape[1]
    nv = GLA_HEADS * 256
    c_ref = math.gcd(seq, GLA_CHUNK)
    if seq >= GLA_CHUNK:
        chunk, rows, n_valid = c_ref, math.gcd(seq, 256), c_ref
        scratch = []
    else:
        chunk, rows, n_valid = GLA_CHUNK, seq, seq
        scratch = [pltpu.VMEM((chunk, n_in), F32)]
    n_l = seq // rows
    p3 = p.reshape(n_batch, seq, n_in)
    has_s0 = s0 is not None
    st_spec = pl.BlockSpec((None, GLA_HEADS, 128, 256), lambda b, l: (b, 0, 0, 0))
    in_specs = [
        pl.BlockSpec((None, rows, n_in), lambda b, l: (b, l, 0)),
        pl.BlockSpec((LANES, GLA_HEADS * 128), lambda b, l: (0, 0)),
        pl.BlockSpec((1, GLA_HEADS * 128), lambda b, l: (0, 0)),
        pl.BlockSpec((1, 256), lambda b, l: (0, 0)),
    ]
    args = [p3, w_gate2_p, b_gate.reshape(1, -1), onorm_g.reshape(1, -1)]
    if has_s0:
        in_specs.append(st_spec)
        args.append(s0)
    o, s_fin = pl.pallas_call(
        functools.partial(_gla_kernel, chunk=chunk, n_chunk=max(rows // chunk, 1), n_valid=n_valid,
                          has_s0=has_s0),
        grid=(n_batch, n_l),
        in_specs=in_specs,
        out_specs=[pl.BlockSpec((None, rows, nv), lambda b, l: (b, l, 0)), st_spec],
        out_shape=[jax.ShapeDtypeStruct((n_batch, seq, nv), F32),
                   jax.ShapeDtypeStruct((n_batch, GLA_HEADS, 128, 256), F32)],
        scratch_shapes=[pltpu.VMEM((GLA_HEADS, 128, 256), F32)] + scratch,
        compiler_params=_cp("parallel", "arbitrary"),
        name="gla",
    )(*args)
    return o.reshape(n_batch * seq, nv), s_fin


def _banded_kernel(q_ref, kc_ref, kp_ref, vc_ref, vp_ref, *rest, hk, grp, dil, window, has_sink, want_lse):
    rest = list(rest)
    sink_ref = rest.pop(0) if has_sink else None
    o_ref = rest.pop(0)
    lse_ref = rest.pop(0) if want_lse else None
    n = pl.program_id(1)
    i = lax.broadcasted_iota(jnp.int32, (2 * BLK, BLK), 0) & (BLK - 1)
    j = lax.broadcasted_iota(jnp.int32, (2 * BLK, BLK), 1)
    valid_c = (j <= i) & (i - j <= window)
    valid_p = i - j + BLK + jnp.where(n > 0, 0, 2 * BLK) <= window
    lo = lax.broadcasted_iota(jnp.int32, (BLK, LANES), 1) < HEAD_DIM
    top = lax.broadcasted_iota(jnp.int32, (2 * BLK, 1), 0) < BLK
    n_tile = q_ref.shape[1] // LANES
    for r in range(dil):
        rows = pl.ds(r, BLK, stride=dil) if dil > 1 else slice(None)
        for t in range(n_tile):
            kv_head = 2 * t // grp
            cols = slice((kv_head // 2) * LANES, (kv_head // 2 + 1) * LANES)
            kv = [ref[rows, cols] for ref in (kc_ref, kp_ref, vc_ref, vp_ref)]
            if grp > 1:
                own_lo = kv_head % 2 == 0
                kv = [jnp.where(lo, x if own_lo else pltpu.roll(x, HEAD_DIM, axis=1),
                                pltpu.roll(x, HEAD_DIM, axis=1) if own_lo else x) for x in kv]
            kc, kp, vc, vp = [x.astype(BF16) for x in kv]
            q2 = q_ref[rows, t * LANES:(t + 1) * LANES]
            qbd = jnp.concatenate([jnp.where(lo, q2, 0.0), jnp.where(lo, 0.0, q2)], axis=0).astype(BF16)
            sc = jnp.where(valid_c, _dot_nt(qbd, kc), NEG)
            sp = jnp.where(valid_p, _dot_nt(qbd, kp), NEG)
            m = jnp.max(jnp.maximum(sc, sp), axis=-1, keepdims=True)
            if has_sink:
                sk = jnp.where(top, sink_ref[:, 2 * t:2 * t + 1], sink_ref[:, 2 * t + 1:2 * t + 2])
                m = jnp.maximum(m, sk)
            pc = jnp.exp(sc - m)
            pp = jnp.exp(sp - m)
            l = jnp.sum(pc + pp, axis=-1, keepdims=True)
            if has_sink:
                l = l + jnp.exp(sk - m)
            pv = (_dot(pc.astype(BF16), vc) + _dot(pp.astype(BF16), vp)) / l
            o_ref[rows, t * LANES:(t + 1) * LANES] = jnp.where(lo, pv[:BLK], pv[BLK:])
            if want_lse:
                lse = m + jnp.log(l)
                lse_ref[rows, t * LANES:(t + 1) * LANES] = jnp.where(
                    lo, jnp.broadcast_to(lse[:BLK], (BLK, LANES)), jnp.broadcast_to(lse[BLK:], (BLK, LANES)))


def _banded(p, n_batch, seq, dil, q_blk, k_blk, v_blk, hk, grp, window, sinks=None, want_lse=False):
    n_in = p.shape[1]
    rows = dil * BLK
    assert seq % rows == 0
    nb = seq // rows
    p3 = p.reshape(n_batch, seq, n_in)
    qw, qi = q_blk
    kw, ki = k_blk
    vw, vi = v_blk
    if dil > 1:
        assert grp == 1 and qw == kw == vw and qw % LANES == 0
        n_cb = qw // LANES
        qi, ki, vi = qi * n_cb, ki * n_cb, vi * n_cb
        qw = kw = vw = LANES
        hk = LANES // HEAD_DIM
    else:
        n_cb = 1

    def cur(w, ci):
        return pl.BlockSpec((None, rows, w), lambda b, n, c: (b, n, ci + c))

    def prev(w, ci):
        return pl.BlockSpec((None, rows, w), lambda b, n, c: (b, jnp.maximum(n - 1, 0), ci + c))

    in_specs = [cur(qw, qi), cur(kw, ki), prev(kw, ki), cur(vw, vi), prev(vw, vi)]
    args = [p3] * 5
    if sinks is not None:
        in_specs.append(pl.BlockSpec((1, sinks.shape[0]), lambda b, n, c: (0, 0)))
        args.append(sinks.reshape(1, -1).astype(F32))
    o_spec = pl.BlockSpec((None, rows, qw), lambda b, n, c: (b, n, c))
    o_shape = jax.ShapeDtypeStruct((n_batch, seq, n_cb * qw), F32)
    outs = pl.pallas_call(
        functools.partial(_banded_kernel, hk=hk, grp=grp, dil=dil, window=window,
                          has_sink=sinks is not None, want_lse=want_lse),
        grid=(n_batch, nb, n_cb),
        in_specs=in_specs,
        out_specs=[o_spec, o_spec] if want_lse else [o_spec],
        out_shape=[o_shape, o_shape] if want_lse else [o_shape],
        compiler_params=_cp("parallel", "arbitrary", "arbitrary"),
        name="banded_attn",
    )(*args)
    return [t.reshape(n_batch * seq, n_cb * qw) for t in outs]


def _kv_tail_kernel(k_ref, v_ref, o_ref):
    w = k_ref.shape[1]
    o_ref[:w, :] = k_ref[...].T
    o_ref[w:, :] = v_ref[...].T


def _kv_tail(p, n_batch, seq, c0, hk, keep):
    n_in = p.shape[1]
    w = hk * HEAD_DIM
    rows = math.gcd(keep, 512)
    assert c0 % w == 0 and rows % LANES == 0 and (seq - keep) % rows == 0
    r0 = (seq - keep) // rows
    p3 = p.reshape(n_batch, seq, n_in)
    t = pl.pallas_call(
        _kv_tail_kernel,
        grid=(n_batch, keep // rows),
        in_specs=[pl.BlockSpec((None, rows, w), lambda b, i: (b, r0 + i, c0 // w)),
                  pl.BlockSpec((None, rows, w), lambda b, i: (b, r0 + i, c0 // w + 1))],
        out_specs=pl.BlockSpec((None, 2 * w, rows), lambda b, i: (b, 0, i)),
        out_shape=jax.ShapeDtypeStruct((n_batch, 2 * w, keep), F32),
        compiler_params=_cp("parallel", "parallel"),
        name="kv_tail",
    )(p3, p3)
    return jnp.transpose(t.reshape(n_batch, 2, hk, HEAD_DIM, keep), (0, 4, 1, 2, 3))


def _win_decode_kernel(q_ref, kn_ref, vn_ref, c_ref, *rest, bb, hk, grp, dil, win, has_sink, want_lse):
    rest = list(rest)
    sink_ref = rest.pop(0) if has_sink else None
    o_ref = rest.pop(0)
    lse_ref = rest.pop(0) if want_lse else None
    nc_ref, pad_ref = rest
    s_len = q_ref.shape[1]
    lb = c_ref.shape[2]
    dh = HEAD_DIM
    kvw = hk * dh
    n_row = grp * s_len
    new0 = LANES - s_len
    s_c = lax.broadcasted_iota(jnp.int32, (n_row, lb), 0) % s_len
    dist_c = lb + s_c - lax.broadcasted_iota(jnp.int32, (n_row, lb), 1)
    valid_c = (dist_c <= win) & ((dist_c & (dil - 1)) == 0)
    s_n = lax.broadcasted_iota(jnp.int32, (n_row, LANES), 0) % s_len
    c_n = lax.broadcasted_iota(jnp.int32, (n_row, LANES), 1) - new0
    dist_n = s_n - c_n
    valid_n = (c_n >= 0) & (dist_n >= 0) & (dist_n <= win) & ((dist_n & (dil - 1)) == 0)
    is_new = lax.broadcasted_iota(jnp.int32, (dh, LANES), 1) >= new0
    pad_ref[:, :new0, :] = jnp.zeros((bb, new0, 2 * kvw), F32)
    for b in range(bb):
        pad_ref[b, new0:, :kvw] = kn_ref[b]
        pad_ref[b, new0:, kvw:] = vn_ref[b]
        new_t = pad_ref[b].T
        for h in range(hk):
            k_rows = slice(h * dh, (h + 1) * dh)
            v_rows = slice(kvw + h * dh, kvw + (h + 1) * dh)
            k_t = c_ref[b, k_rows, :]
            v_t = c_ref[b, v_rows, :]
            kn_t = new_t[k_rows]
            vn_t = new_t[v_rows]
            q = jnp.concatenate([q_ref[b, :, (h * grp + g) * dh:(h * grp + g + 1) * dh] for g in range(grp)],
                                axis=0)
            sc = jnp.where(valid_c, _dot(q.astype(BF16), k_t.astype(BF16)), NEG)
            sn = jnp.where(valid_n, _dot(q, kn_t), NEG)
            m = jnp.maximum(jnp.max(sc, axis=-1, keepdims=True), jnp.max(sn, axis=-1, keepdims=True))
            if has_sink:
                sk = jnp.concatenate(
                    [jnp.broadcast_to(sink_ref[:, h * grp + g:h * grp + g + 1], (s_len, 1)) for g in range(grp)],
                    axis=0)
                m = jnp.maximum(m, sk)
            pc = jnp.exp(sc - m)
            pn = jnp.exp(sn - m)
            l = jnp.sum(pc, axis=-1, keepdims=True) + jnp.sum(pn, axis=-1, keepdims=True)
            if has_sink:
                l = l + jnp.exp(sk - m)
            o = (_dot_nt(pc.astype(BF16), v_t.astype(BF16)) + _dot_nt(pn, vn_t)) / l
            for g in range(grp):
                qs = slice((h * grp + g) * dh, (h * grp + g + 1) * dh)
                o_ref[b, :, qs] = o[g * s_len:(g + 1) * s_len]
                if want_lse:
                    lse_ref[b, :, qs] = jnp.broadcast_to((m + jnp.log(l))[g * s_len:(g + 1) * s_len], (s_len, dh))
            for rows, x_t, x_new in ((k_rows, k_t, kn_t), (v_rows, v_t, vn_t)):
                moved = pltpu.roll(x_t, lb - s_len, axis=1)
                nc_ref[b, rows, :] = moved
                nc_ref[b, rows, lb - LANES:] = jnp.where(is_new, x_new, moved[:, lb - LANES:])


def _win_decode(p, n_batch, s_len, buf, q_blk, k_blk, v_blk, hk, grp, dil, win, sinks=None, want_lse=False):
    n_in = p.shape[1]
    lb = buf.shape[1]
    assert lb == win and lb % LANES == 0 and dil & (dil - 1) == 0 and s_len <= LANES
    dh = HEAD_DIM
    qw, kvw = hk * grp * dh, hk * dh
    buf_t = jnp.transpose(buf, (0, 2, 3, 4, 1)).reshape(n_batch, 2 * kvw, lb)
    bb = max(1, min(math.gcd(n_batch, 8), (2 * 1024 * 1024) // (lb * 2 * kvw * 4)))
    p3 = p.reshape(n_batch, s_len, n_in)
    (qwid, qi), (kwid, ki), (vwid, vi) = q_blk, k_blk, v_blk
    assert qwid == qw and kwid == kvw and vwid == kvw
    q_spec = pl.BlockSpec((bb, s_len, qw), lambda i: (i, 0, 0))
    buf_spec = pl.BlockSpec((bb, 2 * kvw, lb), lambda i: (i, 0, 0))
    in_specs = [
        pl.BlockSpec((bb, s_len, qw), lambda i: (i, 0, qi)),
        pl.BlockSpec((bb, s_len, kvw), lambda i: (i, 0, ki)),
        pl.BlockSpec((bb, s_len, kvw), lambda i: (i, 0, vi)),
        buf_spec,
    ]
    args = [p3, p3, p3, buf_t]
    if sinks is not None:
        in_specs.append(pl.BlockSpec((1, hk * grp), lambda i: (0, 0)))
        args.append(sinks.reshape(1, -1).astype(F32))
    q_shape = jax.ShapeDtypeStruct((n_batch, s_len, qw), F32)
    outs = pl.pallas_call(
        functools.partial(_win_decode_kernel, bb=bb, hk=hk, grp=grp, dil=dil, win=win,
                          has_sink=sinks is not None, want_lse=want_lse),
        grid=(n_batch // bb,),
        in_specs=in_specs,
        out_specs=[q_spec] * (2 if want_lse else 1) + [buf_spec],
        out_shape=[q_shape] * (2 if want_lse else 1) + [jax.ShapeDtypeStruct(buf_t.shape, F32)],
        scratch_shapes=[pltpu.VMEM((bb, LANES, 2 * kvw), F32)],
        compiler_params=_cp("parallel"),
        name="win_decode",
    )(*args)
    nbuf = jnp.transpose(outs[-1].reshape(n_batch, 2, hk, dh, lb), (0, 4, 1, 2, 3))
    return [t.reshape(n_batch * s_len, qw) for t in outs[:-1]] + [nbuf]


def _diff_lambda(lam_ref):
    lam = lam_ref[...]
    a = jnp.sum(lam[0:1] * lam[1:2], axis=-1, keepdims=True)
    b = jnp.sum(lam[2:3] * lam[3:4], axis=-1, keepdims=True)
    return jnp.exp(a) - jnp.exp(b) + LAMBDA_INIT


def _diff_head_out(a1, a2, lam, sub_g):
    o = a1 - lam * a2
    return o * lax.rsqrt(jnp.mean(o * o, axis=-1, keepdims=True) + EPS) * sub_g * (1.0 - LAMBDA_INIT)


def _diff_prompt_kernel(q_ref, kv_ref, lam_ref, sub_ref, o_ref, qb_ref, m_ref, l_ref, acc_ref, *, tq):
    qi = pl.program_id(1)
    dh = HEAD_DIM
    dv = 2 * dh
    kw = DIFF_KV_HEADS * dv
    rows2 = DIFF_GROUP * tq
    r_i = lax.broadcasted_iota(jnp.int32, (rows2, tq), 0) % tq
    c_i = lax.broadcasted_iota(jnp.int32, (rows2, tq), 1)
    diag = c_i <= r_i
    lam = _diff_lambda(lam_ref)
    for h in range(DIFF_KV_HEADS):
        for mp in range(2):
            qb_ref[h * 2 + mp] = jnp.concatenate(
                [q_ref[:, ((h * DIFF_GROUP + g) * 2 + mp) * dh:((h * DIFF_GROUP + g) * 2 + mp + 1) * dh]
                 for g in range(DIFF_GROUP)], axis=0).astype(BF16)
    m_ref[...] = jnp.full_like(m_ref, NEG)
    l_ref[...] = jnp.zeros_like(l_ref)
    acc_ref[...] = jnp.zeros_like(acc_ref)

    def step(kb, carry, masked):
        r0 = pl.multiple_of(kb * tq, tq)
        n_map = DIFF_KV_HEADS * 2
        old = [(m_ref[c], l_ref[c], acc_ref[c]) for c in range(n_map)]
        new = []
        for h in range(DIFF_KV_HEADS):
            v = kv_ref[pl.ds(r0, tq), kw + h * dv:kw + (h + 1) * dv].astype(BF16)
            for mp in range(2):
                m_old, l_old, acc_old = old[h * 2 + mp]
                k = kv_ref[pl.ds(r0, tq), h * dv + mp * dh:h * dv + (mp + 1) * dh].astype(BF16)
                s = _dot_nt(qb_ref[h * 2 + mp], k)
                if masked:
                    s = jnp.where(diag, s, NEG)
                parts = [s[:, i * LANES:(i + 1) * LANES] for i in range(tq // LANES)]
                mx = parts[0]
                for t in parts[1:]:
                    mx = jnp.maximum(mx, t)
                m_new = jnp.maximum(m_old, jnp.max(mx, axis=-1, keepdims=True))
                corr = jnp.exp(m_old - m_new)
                ps = [jnp.exp(t - m_new) for t in parts]
                l_new = l_old * corr
                for t in ps:
                    l_new = l_new + t
                p = jnp.concatenate(ps, axis=1).astype(BF16)
                new.append((m_new, l_new, acc_old * corr + _dot(p, v)))
        for c in range(n_map):
            m_ref[c], l_ref[c], acc_ref[c] = new[c]
        return carry

    lax.fori_loop(0, qi, functools.partial(step, masked=False), 0)
    step(qi, 0, True)
    for h in range(DIFF_KV_HEADS):
        res = [acc_ref[h * 2 + mp] / jnp.sum(l_ref[h * 2 + mp], axis=-1, keepdims=True) for mp in range(2)]
        for g in range(DIFF_GROUP):
            hq = h * DIFF_GROUP + g
            o_ref[:, hq * dv:(hq + 1) * dv] = _diff_head_out(res[0][g * tq:(g + 1) * tq], res[1][g * tq:(g + 1) * tq],
                                                             lam, sub_ref[...])


def _diff_prompt(p, n_batch, seq, lam_vec, subln_g):
    n_in = p.shape[1]
    nq = DIFF_KV_HEADS * DIFF_GROUP * 2 * HEAD_DIM
    tq = math.gcd(seq, 256)
    n_map = DIFF_KV_HEADS * 2
    assert 2 * HEAD_DIM == LANES and tq % LANES == 0
    p3 = p.reshape(n_batch, seq, n_in)
    o = pl.pallas_call(
        functools.partial(_diff_prompt_kernel, tq=tq),
        grid=(n_batch, seq // tq),
        in_specs=[
            pl.BlockSpec((None, tq, nq), lambda b, i: (b, i, 0)),
            pl.BlockSpec((None, seq, nq), lambda b, i: (b, 0, 1)),
            pl.BlockSpec((4, HEAD_DIM), lambda b, i: (0, 0)),
            pl.BlockSpec((1, 2 * HEAD_DIM), lambda b, i: (0, 0)),
        ],
        out_specs=pl.BlockSpec((None, tq, nq), lambda b, i: (b, i, 0)),
        out_shape=jax.ShapeDtypeStruct((n_batch, seq, nq), F32),
        scratch_shapes=[pltpu.VMEM((n_map, DIFF_GROUP * tq, HEAD_DIM), BF16)]
        + [pltpu.VMEM((n_map, DIFF_GROUP * tq, LANES), F32)] * 3,
        compiler_params=_cp("parallel", "arbitrary"),
        name="diff_attn",
    )(p3, p3, lam_vec.astype(F32), subln_g.reshape(1, -1).astype(F32))
    return o.reshape(n_batch * seq, nq)


def _diff_sample_kernel(pt_ref, q_ref, kv_ref, lam_ref, sub_ref, *rest, pps):
    pages = rest[:pps]
    o_ref, qb_ref, m_ref, l_ref, acc_ref, nk_ref = rest[pps:]
    step = pl.program_id(1)
    n_step = pl.num_programs(1)
    s_len = q_ref.shape[0]
    dh = HEAD_DIM
    dv = 2 * dh
    kw = DIFF_KV_HEADS * dv
    hr = DIFF_GROUP * 2 * s_len
    stride = 2 * DIFF_KV_HEADS

    @pl.when(step == 0)
    def _():
        qb_ref[...] = jnp.zeros_like(qb_ref)
        for idx in range(DIFF_KV_HEADS * DIFF_GROUP * 2):
            mp = idx % 2
            qb_ref[idx * s_len:(idx + 1) * s_len, mp * dh:(mp + 1) * dh] = q_ref[:, idx * dh:(idx + 1) * dh]
        m_ref[...] = jnp.full_like(m_ref, NEG)
        l_ref[...] = jnp.zeros_like(l_ref)
        acc_ref[...] = jnp.zeros_like(acc_ref)

    def update(parts_h, v_h):
        heads = range(DIFF_KV_HEADS)
        old = [(m_ref[h * hr:(h + 1) * hr], l_ref[h * hr:(h + 1) * hr], acc_ref[h * hr:(h + 1) * hr]) for h in heads]
        new = []
        for h in heads:
            m_old, l_old, acc_old = old[h]
            parts = parts_h[h]
            mx = parts[0]
            for t in parts[1:]:
                mx = jnp.maximum(mx, t)
            m_new = jnp.maximum(m_old, jnp.max(mx, axis=-1, keepdims=True))
            corr = jnp.exp(m_old - m_new)
            ps = [jnp.exp(t - m_new) for t in parts]
            l_new = l_old * corr
            for t in ps:
                l_new = l_new + t
            p = ps[0] if len(ps) == 1 else jnp.concatenate(ps, axis=1)
            new.append((m_new, l_new, acc_old * corr + _dot(p.astype(BF16), v_h[h])))
        for h in heads:
            m_ref[h * hr:(h + 1) * hr], l_ref[h * hr:(h + 1) * hr], acc_ref[h * hr:(h + 1) * hr] = new[h]

    qhs = [qb_ref[h * hr:(h + 1) * hr, :].astype(BF16) for h in range(DIFF_KV_HEADS)]
    update([[_dot_nt(qhs[h], pg[pl.ds(h, PAGE_SIZE, stride=stride), :].astype(BF16)) for pg in pages]
            for h in range(DIFF_KV_HEADS)],
           [jnp.concatenate([pg[pl.ds(DIFF_KV_HEADS + h, PAGE_SIZE, stride=stride), :] for pg in pages],
                            axis=0).astype(BF16) for h in range(DIFF_KV_HEADS)])

    @pl.when(step == n_step - 1)
    def _():
        nk_ref[...] = jnp.zeros_like(nk_ref)
        nk_ref[:s_len, :] = kv_ref[...]
        r_s = lax.broadcasted_iota(jnp.int32, (hr, PAGE_SIZE), 0) % s_len
        c_s = lax.broadcasted_iota(jnp.int32, (hr, PAGE_SIZE), 1)
        update([[jnp.where(c_s <= r_s, _dot_nt(qhs[h], nk_ref[:, h * dv:(h + 1) * dv].astype(BF16)), NEG)]
                for h in range(DIFF_KV_HEADS)],
               [nk_ref[:, kw + h * dv:kw + (h + 1) * dv].astype(BF16) for h in range(DIFF_KV_HEADS)])
        lam = _diff_lambda(lam_ref)
        res = acc_ref[...] / jnp.sum(l_ref[...], axis=-1, keepdims=True)
        for hq in range(DIFF_KV_HEADS * DIFF_GROUP):
            a1 = res[(hq * 2) * s_len:(hq * 2 + 1) * s_len]
            a2 = res[(hq * 2 + 1) * s_len:(hq * 2 + 2) * s_len]
            o_ref[:, hq * dv:(hq + 1) * dv] = _diff_head_out(a1, a2, lam, sub_ref[...])


def _diff_sample(p, n_batch, s_len, cache, page_table, lam_vec, subln_g):
    n_in = p.shape[1]
    nq = DIFF_KV_HEADS * DIFF_GROUP * 2 * HEAD_DIM
    n_pages = page_table.shape[1]
    pps = math.gcd(n_pages, 16)
    n_rows = DIFF_KV_HEADS * DIFF_GROUP * 2 * s_len
    p3 = p.reshape(n_batch, s_len, n_in)
    kvw = 2 * DIFF_KV_HEADS * 2 * HEAD_DIM
    assert 2 * HEAD_DIM == LANES and PAGE_SIZE == LANES
    page_rows = PAGE_SIZE * 2 * DIFF_KV_HEADS
    cache3 = cache.reshape(cache.shape[0] * page_rows, LANES)

    def page_spec(jj):
        return pl.BlockSpec((page_rows, LANES), lambda b, s, pt: (pt[b * n_pages + s * pps + jj], 0))

    grid_spec = pltpu.PrefetchScalarGridSpec(
        num_scalar_prefetch=1,
        grid=(n_batch, n_pages // pps),
        in_specs=[
            pl.BlockSpec((None, s_len, nq), lambda b, s, pt: (b, 0, 0)),
            pl.BlockSpec((None, s_len, nq), lambda b, s, pt: (b, 0, 1)),
            pl.BlockSpec((4, HEAD_DIM), lambda b, s, pt: (0, 0)),
            pl.BlockSpec((1, 2 * HEAD_DIM), lambda b, s, pt: (0, 0)),
        ] + [page_spec(jj) for jj in range(pps)],
        out_specs=pl.BlockSpec((None, s_len, nq), lambda b, s, pt: (b, 0, 0)),
        scratch_shapes=[pltpu.VMEM((n_rows, LANES), F32)] * 4 + [pltpu.VMEM((PAGE_SIZE, kvw), F32)],
    )
    o = pl.pallas_call(
        functools.partial(_diff_sample_kernel, pps=pps),
        grid_spec=grid_spec,
        out_shape=jax.ShapeDtypeStruct((n_batch, s_len, nq), F32),
        compiler_params=_cp("parallel", "arbitrary"),
        name="diff_decode",
    )(page_table.reshape(-1), p3, p3, lam_vec.astype(F32), subln_g.reshape(1, -1).astype(F32),
      *([cache3] * pps))
    return o.reshape(n_batch * s_len, nq)


def kernel(x_prompt, x_sample, state_gla, cache_swa, cache_dil1, cache_dil2, cache_dil3, cache_diff, page_table, c_prompt, c_sample, norm1_g, norm2_g, ada_w, ada_b, gla_w_in, gla_w_gate2, gla_b_gate, gla_onorm_g, gla_w_out, swa_w_in, swa_q_norm, swa_k_norm, swa_sinks, swa_w_out, dil_w_in, dil_q_norm, dil_k_norm, dil_w_out, diff_w_in, diff_q_norm, diff_k_norm, diff_lambda, diff_subln_g, diff_w_out, ffn_w_gu, ffn_w_down, moe_router, moe_w_gu, moe_w_down):
    bp, seq, d = x_prompt.shape
    db, s_len, _ = x_sample.shape
    depth = ada_w.shape[0]
    dh = HEAD_DIM
    qscale = dh ** -0.5

    mods = _ada(jnp.concatenate([c_prompt, c_sample], axis=0), ada_w, ada_b)
    xp = x_prompt.reshape(bp * seq, d)
    xs = x_sample.reshape(db * s_len, d)
    out = {}

    for i in range(depth):
        gp = _Group(bp, seq, mods[i, :bp])
        gs = _Group(db, s_len, mods[i, bp:])
        kind = i % 4
        if kind == 0:
            n_in = gla_w_in.shape[1]
            n_pad = -(-n_in // (5 * LANES)) * (5 * LANES)
            w_in = jnp.pad(gla_w_in, ((0, 0), (0, n_pad - n_in)))
            lr_w = n_pad - (n_in - GLA_RANK)
            wg2 = jnp.pad(gla_w_gate2, ((0, lr_w - GLA_RANK), (0, 0)))
            pp = _inproj(gp, xp, norm1_g[i], 0, w_in, n_pad // 5)
            ps = _inproj(gs, xs, norm1_g[i], 0, w_in, n_pad // 5)
            op, out["gla_p"] = _gla(pp, bp, seq, wg2, gla_b_gate, gla_onorm_g, None)
            os_, out["gla_s"] = _gla(ps, db, s_len, wg2, gla_b_gate, gla_onorm_g, state_gla)
            xp = _outproj(gp, xp, op, gla_w_out, 2)
            xs = _outproj(gs, xs, os_, gla_w_out, 2)
        elif kind == 1:
            hk = SWA_KV_HEADS
            nq = swa_w_out.shape[0]
            grp = nq // dh // hk
            kvw = hk * dh
            norm = _norm_rows([(nq, swa_q_norm, qscale), (kvw, swa_k_norm, 1.0), (kvw, None, 1.0)])
            pp = _inproj(gp, xp, norm1_g[i], 0, swa_w_in, kvw * 2, norm)
            ps = _inproj(gs, xs, norm1_g[i], 0, swa_w_in, kvw * 2, norm)
            (op,) = _banded(pp, bp, seq, 1, (nq, 0), (kvw, nq // kvw), (kvw, nq // kvw + 1),
                            hk, grp, SWA_WINDOW, sinks=swa_sinks)
            keep = min(SWA_WINDOW, seq)
            out["swa_p"] = _kv_tail(pp, bp, seq, nq, hk, keep)
            os_, out["swa_s"] = _win_decode(ps, db, s_len, cache_swa, (nq, 0), (kvw, nq // kvw),
                                            (kvw, nq // kvw + 1), hk, grp, 1, SWA_WINDOW, sinks=swa_sinks)
            xp = _outproj(gp, xp, op, swa_w_out, 2)
            xs = _outproj(gs, xs, os_, swa_w_out, 2)
        elif kind == 2:
            w = DIL_HEADS * dh
            norm = _norm_rows([(w, dil_q_norm, qscale), (w, dil_k_norm, 1.0), (w, None, 1.0)] * len(DIL_GROUPS))
            pp = _inproj(gp, xp, norm1_g[i], 0, dil_w_in, w, norm)
            ps = _inproj(gs, xs, norm1_g[i], 0, dil_w_in, w, norm)
            outs_p, lses_p, outs_s, lses_s = [], [], [], []
            caches = (cache_dil1, cache_dil2, cache_dil3)
            for gi, (win, dil) in enumerate(DIL_GROUPS):
                o, lse = _banded(pp, bp, seq, dil, (w, 3 * gi), (w, 3 * gi + 1), (w, 3 * gi + 2),
                                 DIL_HEADS, 1, win // dil, want_lse=True)
                outs_p.append(o)
                lses_p.append(lse)
                keep = min(win, seq)
                out["dil%d_p" % gi] = _kv_tail(pp, bp, seq, (3 * gi + 1) * w, DIL_HEADS, keep)
                o, lse, out["dil%d_s" % gi] = _win_decode(ps, db, s_len, caches[gi], (w, 3 * gi), (w, 3 * gi + 1),
                                                          (w, 3 * gi + 2), DIL_HEADS, 1, dil, win, want_lse=True)
                outs_s.append(o)
                lses_s.append(lse)
            xp = _outproj_mix(gp, xp, outs_p, lses_p, dil_w_out, 2)
            xs = _outproj_mix(gs, xs, outs_s, lses_s, dil_w_out, 2)
        else:
            nq = DIFF_KV_HEADS * DIFF_GROUP * 2 * dh
            nk = DIFF_KV_HEADS * 2 * dh
            norm = _norm_rows([(nq, diff_q_norm, qscale), (nk, diff_k_norm, 1.0), (nk, None, 1.0)])
            pp = _inproj(gp, xp, norm1_g[i], 0, diff_w_in, nk, norm)
            ps = _inproj(gs, xs, norm1_g[i], 0, diff_w_in, nk, norm)
            op = _diff_prompt(pp, bp, seq, diff_lambda, diff_subln_g)
            os_ = _diff_sample(ps, db, s_len, cache_diff, page_table, diff_lambda, diff_subln_g)
            out["diff_p"] = pp[:, nq:].reshape(bp, seq, 2, DIFF_KV_HEADS, 2 * dh)
            out["diff_s"] = ps[:, nq:].reshape(db, s_len, 2, DIFF_KV_HEADS, 2 * dh)
            xp = _outproj(gp, xp, op, diff_w_out, 2)
            xs = _outproj(gs, xs, os_, diff_w_out, 2)
        j = i // 2
        if i % 2 == 0:
            w_gu, w_down = ffn_w_gu[j].astype(BF16), ffn_w_down[j].astype(BF16)
            xp = _ffn(gp, xp, norm2_g[i], w_gu, w_down, 1408)
            xs = _ffn(gs, xs, norm2_g[i], w_gu, w_down, 1408)
        else:
            xp = _moe(gp, xp, norm2_g[i], moe_router[j], moe_w_gu, moe_w_down, j, 512)
            xs = _moe(gs, xs, norm2_g[i], moe_router[j], moe_w_gu, moe_w_down, j, 512)

    return (xp.reshape(bp, seq, d), xs.reshape(db, s_len, d), out["gla_p"], out["gla_s"],
            out["swa_p"], out["swa_s"], out["dil0_p"], out["dil0_s"], out["dil1_p"], out["dil1_s"],
            out["dil2_p"], out["dil2_s"], out["diff_p"], out["diff_s"])
```

```python
import functools
import math

import jax
import jax.numpy as jnp
from jax import lax
from jax.experimental import pallas as pl
from jax.experimental.pallas import tpu as pltpu

F32 = jnp.float32
BF16 = jnp.bfloat16

EPS = 1e-6
NEG = -1e30
BLK = 128
HEAD_DIM = 64

GLA_HEADS = 4
GLA_RANK = 16
GLA_NORMALIZER = 16.0
GLA_CHUNK = 64

SWA_KV_HEADS = 4
SWA_WINDOW = 128
DIL_GROUPS = ((128, 1), (512, 4), (2048, 16))
DIL_HEADS = 8
DIFF_KV_HEADS = 4
DIFF_GROUP = 2
LAMBDA_INIT = 0.8 - 0.6 * math.exp(-0.3 * 3)
N_EXPERTS = 8
PAGE_SIZE = 128

LANES = 128
VMEM_LIMIT = 56 * 1024 * 1024


def _cp(*sem):
    return pltpu.CompilerParams(dimension_semantics=sem, vmem_limit_bytes=VMEM_LIMIT)


def _dot(a, b):
    return jnp.dot(a, b, preferred_element_type=F32)


def _dot_nt(a, b):
    return lax.dot_general(a, b, (((1,), (1,)), ((), ())), preferred_element_type=F32)


def _dot_tn(a, b):
    return lax.dot_general(a, b, (((0,), (0,)), ((), ())), preferred_element_type=F32)


def _silu(x):
    return x / (1.0 + jnp.exp(-x))


def _adaln(x, g, shift, scale):
    y = x * lax.rsqrt(jnp.mean(x * x, axis=-1, keepdims=True) + EPS)
    return y * g * (1.0 + scale) + shift


class _Group:
    def __init__(self, n_batch, rows_per_batch, mods):
        self.m = n_batch * rows_per_batch
        self.rows_per_batch = rows_per_batch
        self.prompt = rows_per_batch >= 256
        d6 = mods.shape[-1]
        if self.prompt:
            self.mods = mods.reshape(n_batch, 1, d6)
        else:
            self.mods = jnp.repeat(mods, rows_per_batch, axis=0)
        self.d = d6 // 6
        self.tm = self.tile(1024)

    def tile(self, tm_max):
        return math.gcd(self.rows_per_batch if self.prompt else self.m, tm_max)

    def mod_spec(self, k, tm=None):
        d = self.d
        tm = tm or self.tm
        if self.prompt:
            tpb = self.rows_per_batch // tm
            return pl.BlockSpec((None, 1, d), lambda i, *_: (i // tpb, 0, k))
        return pl.BlockSpec((tm, d), lambda i, *_: (i, k))


def _ada_kernel(c_ref, w_ref, b_ref, o_ref):
    a = _silu(c_ref[...]).astype(BF16)
    o_ref[...] = _dot(a, w_ref[...].astype(BF16)) + b_ref[...]


def _ada(c_all, ada_w, ada_b):
    depth, d, d6 = ada_w.shape
    nb = c_all.shape[0]
    tn = 1024
    return pl.pallas_call(
        _ada_kernel,
        grid=(depth, d6 // tn),
        in_specs=[
            pl.BlockSpec((nb, d), lambda l, j: (0, 0)),
            pl.BlockSpec((None, d, tn), lambda l, j: (l, 0, j)),
            pl.BlockSpec((None, 1, tn), lambda l, j: (l, 0, j)),
        ],
        out_specs=pl.BlockSpec((None, nb, tn), lambda l, j: (l, 0, j)),
        out_shape=jax.ShapeDtypeStruct((depth, nb, d6), F32),
        compiler_params=_cp("parallel", "parallel"),
        name="ada_mod",
    )(c_all, ada_w, ada_b.reshape(depth, 1, d6))


def _inproj_kernel(x_ref, g_ref, sh_ref, sc_ref, w_ref, *rest, norm):
    if norm:
        flag_ref, gain_ref, o_ref, h_ref = rest
    else:
        o_ref, h_ref = rest
    j = pl.program_id(1)

    @pl.when(j == 0)
    def _():
        h_ref[...] = _adaln(x_ref[...], g_ref[...], sh_ref[...], sc_ref[...]).astype(BF16)

    acc = _dot(h_ref[...], w_ref[...].astype(BF16))
    if not norm:
        o_ref[...] = acc
        return
    tn = acc.shape[1]
    lo = lax.broadcasted_iota(jnp.int32, (1, LANES), 1) < HEAD_DIM
    for c in range(tn // LANES):
        sl = slice(c * LANES, (c + 1) * LANES)
        a = acc[:, sl]
        sq = a * a
        s_lo = jnp.sum(jnp.where(lo, sq, 0.0), axis=-1, keepdims=True)
        s_hi = jnp.sum(jnp.where(lo, 0.0, sq), axis=-1, keepdims=True)
        r = lax.rsqrt(jnp.where(lo, s_lo, s_hi) * (1.0 / HEAD_DIM) + EPS)
        o_ref[:, sl] = a * jnp.where(flag_ref[:, sl] > 0.0, r, 1.0) * gain_ref[:, sl]


def _inproj(grp, x, norm_g, k_shift, w, tn, norm=None):
    m, d = x.shape
    n = w.shape[1]
    tm = grp.tm
    in_specs = [
        pl.BlockSpec((tm, d), lambda i, j: (i, 0)),
        pl.BlockSpec((1, d), lambda i, j: (0, 0)),
        grp.mod_spec(k_shift),
        grp.mod_spec(k_shift + 1),
        pl.BlockSpec((d, tn), lambda i, j: (0, j)),
    ]
    args = [x, norm_g.reshape(1, d), grp.mods, grp.mods, w]
    if norm is not None:
        in_specs += [pl.BlockSpec((1, tn), lambda i, j: (0, j))] * 2
        args += list(norm)
    return pl.pallas_call(
        functools.partial(_inproj_kernel, norm=norm is not None),
        grid=(m // tm, n // tn),
        in_specs=in_specs,
        out_specs=pl.BlockSpec((tm, tn), lambda i, j: (i, j)),
        out_shape=jax.ShapeDtypeStruct((m, n), F32),
        scratch_shapes=[pltpu.VMEM((tm, d), BF16)],
        compiler_params=_cp("parallel", "arbitrary"),
        name="adaln_inproj",
    )(*args)


def _norm_rows(pieces):
    flags, gains = [], []
    for width, g, scale in pieces:
        if g is None:
            flags.append(jnp.zeros((width,), F32))
            gains.append(jnp.ones((width,), F32))
        else:
            flags.append(jnp.ones((width,), F32))
            gains.append(jnp.tile(g.astype(F32) * scale, width // HEAD_DIM))
    return jnp.concatenate(flags)[None, :], jnp.concatenate(gains)[None, :]


def _outproj_kernel(x_ref, o_ref, gt_ref, w_ref, out_ref):
    y = _dot(o_ref[...].astype(BF16), w_ref[...].astype(BF16))
    out_ref[...] = x_ref[...] + gt_ref[...] * y


def _outproj(grp, x, o, w, k_gate):
    m, d = x.shape
    kdim = w.shape[0]
    tm = grp.tm
    return pl.pallas_call(
        _outproj_kernel,
        grid=(m // tm,),
        in_specs=[
            pl.BlockSpec((tm, d), lambda i: (i, 0)),
            pl.BlockSpec((tm, kdim), lambda i: (i, 0)),
            grp.mod_spec(k_gate),
            pl.BlockSpec((kdim, d), lambda i: (0, 0)),
        ],
        out_specs=pl.BlockSpec((tm, d), lambda i: (i, 0)),
        out_shape=jax.ShapeDtypeStruct((m, d), F32),
        compiler_params=_cp("parallel"),
        name="outproj",
    )(x, o, grp.mods, w)


def _outproj_mix_kernel(x_ref, o0_ref, o1_ref, o2_ref, l0_ref, l1_ref, l2_ref, gt_ref, w_ref, out_ref):
    l0, l1, l2 = l0_ref[...], l1_ref[...], l2_ref[...]
    mx = jnp.maximum(jnp.maximum(l0, l1), l2)
    e0, e1, e2 = jnp.exp(l0 - mx), jnp.exp(l1 - mx), jnp.exp(l2 - mx)
    o = (e0 * o0_ref[...] + e1 * o1_ref[...] + e2 * o2_ref[...]) / (e0 + e1 + e2)
    y = _dot(o.astype(BF16), w_ref[...].astype(BF16))
    out_ref[...] = x_ref[...] + gt_ref[...] * y


def _outproj_mix(grp, x, outs, lses, w, k_gate):
    m, d = x.shape
    kdim = w.shape[0]
    tm = grp.tm
    row = pl.BlockSpec((tm, kdim), lambda i: (i, 0))
    return pl.pallas_call(
        _outproj_mix_kernel,
        grid=(m // tm,),
        in_specs=[pl.BlockSpec((tm, d), lambda i: (i, 0))] + [row] * 6
        + [grp.mod_spec(k_gate), pl.BlockSpec((kdim, d), lambda i: (0, 0))],
        out_specs=pl.BlockSpec((tm, d), lambda i: (i, 0)),
        out_shape=jax.ShapeDtypeStruct((m, d), F32),
        compiler_params=_cp("parallel"),
        name="outproj_mix",
    )(x, *outs, *lses, grp.mods, w)


def _ffn_kernel(x_ref, g_ref, sh_ref, sc_ref, gt_ref, wg_ref, wu_ref, wd_ref, o_ref, h_ref):
    f = pl.program_id(1)
    nf = pl.num_programs(1)

    @pl.when(f == 0)
    def _():
        h_ref[...] = _adaln(x_ref[...], g_ref[...], sh_ref[...], sc_ref[...]).astype(BF16)
        o_ref[...] = jnp.zeros_like(o_ref)

    h = h_ref[...]
    gate = _dot(h, wg_ref[...])
    up = _dot(h, wu_ref[...])
    a = (_silu(gate) * up).astype(BF16)
    o_ref[...] += _dot(a, wd_ref[...])

    @pl.when(f == nf - 1)
    def _():
        o_ref[...] = x_ref[...] + gt_ref[...] * o_ref[...]


def _ffn(grp, x, norm_g, w_gu, w_down, tf):
    m, d = x.shape
    dff = w_down.shape[0]
    tm = grp.tm
    nf = dff // tf
    return pl.pallas_call(
        _ffn_kernel,
        grid=(m // tm, nf),
        in_specs=[
            pl.BlockSpec((tm, d), lambda i, f: (i, 0)),
            pl.BlockSpec((1, d), lambda i, f: (0, 0)),
            grp.mod_spec(3), grp.mod_spec(4), grp.mod_spec(5),
            pl.BlockSpec((d, tf), lambda i, f: (0, f)),
            pl.BlockSpec((d, tf), lambda i, f: (0, f + nf)),
            pl.BlockSpec((tf, d), lambda i, f: (f, 0)),
        ],
        out_specs=pl.BlockSpec((tm, d), lambda i, f: (i, 0)),
        out_shape=jax.ShapeDtypeStruct((m, d), F32),
        scratch_shapes=[pltpu.VMEM((tm, d), BF16)],
        compiler_params=_cp("parallel", "arbitrary"),
        name="ffn_swiglu",
    )(x, norm_g.reshape(1, d), grp.mods, grp.mods, grp.mods, w_gu, w_gu, w_down)


def _route_top2(logits):
    lane = lax.broadcasted_iota(jnp.int32, logits.shape, 1)
    lg = jnp.where(lane < N_EXPERTS, logits, -jnp.inf)
    m1 = jnp.max(lg, axis=-1, keepdims=True)
    i1 = jnp.min(jnp.where(lg == m1, lane, LANES), axis=-1, keepdims=True)
    lg2 = jnp.where(lane == i1, -jnp.inf, lg)
    m2 = jnp.max(lg2, axis=-1, keepdims=True)
    i2 = jnp.min(jnp.where(lg2 == m2, lane, LANES), axis=-1, keepdims=True)
    e2 = jnp.exp(m2 - m1)
    g1 = 1.0 / (1.0 + e2)
    g2 = e2 / (1.0 + e2)
    comb = jnp.where(lane == i1, g1, 0.0) + jnp.where(lane == i2, g2, 0.0)
    return comb, (lane == i1) | (lane == i2)


def _moe_kernel(x_ref, g_ref, sh_ref, sc_ref, gt_ref, r_ref, wg_ref, wu_ref, wd_ref, o_ref,
                h_ref, comb_ref, key_ref, keyt_ref, cnt_ref, xs_ref, acc_ref, n_ref, *, sub):
    e = pl.program_id(1)
    f = pl.program_id(2)
    nf = pl.num_programs(2)
    tm = h_ref.shape[0]
    lane = lax.broadcasted_iota(jnp.int32, (tm, LANES), 1)

    @pl.when((e == 0) & (f == 0))
    def _():
        x = x_ref[...]
        h = _adaln(x, g_ref[...], sh_ref[...], sc_ref[...]).astype(BF16)
        h_ref[...] = h
        comb, sel = _route_top2(_dot(h, r_ref[...].astype(BF16)))
        comb_ref[...] = comb
        tb = math.gcd(tm, 256)
        tri = jnp.where(lax.broadcasted_iota(jnp.int32, (tb, tb), 0) >= lax.broadcasted_iota(jnp.int32, (tb, tb), 1),
                        1.0, 0.0).astype(BF16)
        carry = jnp.zeros((1, LANES), F32)
        for blk in range(tm // tb):
            rows = slice(blk * tb, (blk + 1) * tb)
            sb = jnp.where(sel[rows], 1.0, 0.0)
            incl = _dot(tri, sb.astype(BF16)) + carry
            key_ref[rows, :] = jnp.where(sb > 0.0, incl - 1.0, -1.0).astype(jnp.int32)
            carry = carry + jnp.sum(sb, axis=0, keepdims=True)
        cnt_ref[...] = carry
        keyt_ref[...] = key_ref[...].T
        o_ref[...] = x

    def onehot(r0, nr):
        keyrow = keyt_ref[pl.ds(e, 1), :]
        r = lax.broadcasted_iota(jnp.int32, (nr, tm), 0) + r0
        return jnp.where(keyrow == r, 1.0, 0.0).astype(BF16)

    def for_blocks(fn):
        def body(s, carry):
            fn(pl.multiple_of(s * sub, sub), sub)
            return carry
        lax.fori_loop(0, (n_e + sub - 1) // sub, body, 0)

    @pl.when(f == 0)
    def _():
        cnt = jnp.sum(jnp.where(lane[:1] == e, cnt_ref[...], 0.0), axis=-1, keepdims=True)
        n_ref[0] = cnt.astype(jnp.int32)[0, 0]

    n_e = n_ref[0]

    @pl.when(f == 0)
    def _():
        def pack(r0, nr):
            xs_ref[pl.ds(r0, nr), :] = _dot(onehot(r0, nr), h_ref[...]).astype(BF16)
            acc_ref[pl.ds(r0, nr), :] = jnp.zeros((nr, acc_ref.shape[1]), F32)
        for_blocks(pack)

    def expert(r0, nr):
        xb = xs_ref[pl.ds(r0, nr), :]
        a = (_silu(_dot(xb, wg_ref[...].astype(BF16))) * _dot(xb, wu_ref[...].astype(BF16))).astype(BF16)
        acc_ref[pl.ds(r0, nr), :] += _dot(a, wd_ref[...].astype(BF16))
    for_blocks(expert)

    @pl.when(f == nf - 1)
    def _():
        ce = jnp.sum(jnp.where(lane == e, comb_ref[...], 0.0), axis=-1, keepdims=True)
        c_hi = ce.astype(BF16).astype(F32)
        c_mid = (ce - c_hi).astype(BF16).astype(F32)
        c_lo = ce - c_hi - c_mid
        cw = jnp.where(lane == 0, c_hi, jnp.where(lane == 1, c_mid, jnp.where(lane == 2, c_lo, 0.0))).astype(BF16)
        kcol = jnp.sum(jnp.where(lane == e, key_ref[...].astype(F32), 0.0), axis=-1,
                       keepdims=True).astype(jnp.int32)

        def unpack(r0, nr):
            g3 = _dot(onehot(r0, nr), cw)
            gate = g3[:, 0:1] + g3[:, 1:2] + g3[:, 2:3]
            y = (acc_ref[pl.ds(r0, nr), :] * gate).astype(BF16)
            c = lax.broadcasted_iota(jnp.int32, (tm, nr), 1) + r0
            scatter = jnp.where(kcol == c, 1.0, 0.0).astype(BF16)
            o_ref[...] += gt_ref[...] * _dot(scatter, y)
        for_blocks(unpack)


def _moe(grp, x, norm_g, router, w_gu, w_down, layer, tf):
    m, d = x.shape
    _, ne, dff, _ = w_down.shape
    tm = grp.tile(2048)
    sub = math.gcd(tm, 256)
    nf = dff // tf
    router_p = jnp.pad(router, ((0, 0), (0, LANES - ne)))
    resident = dict(pipeline_mode=pl.Buffered(1))
    return pl.pallas_call(
        functools.partial(_moe_kernel, sub=sub),
        grid=(m // tm, ne, nf),
        in_specs=[
            pl.BlockSpec((tm, d), lambda i, e, f: (i, 0), **resident),
            pl.BlockSpec((1, d), lambda i, e, f: (0, 0)),
            grp.mod_spec(3, tm), grp.mod_spec(4, tm), grp.mod_spec(5, tm),
            pl.BlockSpec((d, LANES), lambda i, e, f: (0, 0)),
            pl.BlockSpec((None, None, d, tf), lambda i, e, f: (layer, e, 0, f)),
            pl.BlockSpec((None, None, d, tf), lambda i, e, f: (layer, e, 0, f + nf)),
            pl.BlockSpec((None, None, tf, d), lambda i, e, f: (layer, e, f, 0)),
        ],
        out_specs=pl.BlockSpec((tm, d), lambda i, e, f: (i, 0), **resident),
        out_shape=jax.ShapeDtypeStruct((m, d), F32),
        scratch_shapes=[
            pltpu.VMEM((tm, d), BF16), pltpu.VMEM((tm, LANES), F32), pltpu.VMEM((tm, LANES), jnp.int32),
            pltpu.VMEM((LANES, tm), jnp.int32), pltpu.VMEM((1, LANES), F32),
            pltpu.VMEM((tm, d), BF16), pltpu.VMEM((tm, d), F32),
            pltpu.SMEM((1,), jnp.int32),
        ],
        compiler_params=_cp("parallel", "arbitrary", "arbitrary"),
        name="moe",
    )(x, norm_g.reshape(1, d), grp.mods, grp.mods, grp.mods, router_p, w_gu, w_gu, w_down)


def _gla_kernel(p_ref, wg2_ref, bg_ref, on_ref, *rest, chunk, n_chunk, n_valid, has_s0):
    rest = list(rest)
    s0_ref = rest.pop(0) if has_s0 else None
    o_ref, sfin_ref, s_ref = rest[:3]
    if n_valid < chunk:
        src = rest[3]
        src[...] = jnp.zeros_like(src)
        src[:n_valid, :] = p_ref[...]
    else:
        src = p_ref
    li = pl.program_id(1)
    nl = pl.num_programs(1)
    nq = GLA_HEADS * 128
    nv = GLA_HEADS * 256
    dk, dv = 128, 256

    @pl.when(li == 0)
    def _():
        if has_s0:
            s_ref[...] = s0_ref[...]
        else:
            s_ref[...] = jnp.zeros_like(s_ref)

    c = chunk
    row = lax.broadcasted_iota(jnp.int32, (c, c), 0)
    col = lax.broadcasted_iota(jnp.int32, (c, c), 1)
    causal = row >= col
    tri = jnp.where(causal, 1.0, 0.0).astype(BF16)
    ones = jnp.ones((c, dv), BF16)
    live = None
    if n_valid < c:
        live = lax.broadcasted_iota(jnp.int32, (c, 1), 0) < n_valid
    for ci in range(n_chunk):
        rows = slice(ci * c, (ci + 1) * c)
        a_lr = src[rows, 2 * nq + 2 * nv:]
        z = _dot(a_lr.astype(BF16), wg2_ref[...].astype(BF16)) + bg_ref[...]
        la = (jnp.minimum(z, 0.0) - jnp.log1p(jnp.exp(-jnp.abs(z)))) * (1.0 / GLA_NORMALIZER)
        if live is not None:
            la = jnp.where(live, la, 0.0)
        hi = la.astype(BF16)
        lo = (la - hi.astype(F32)).astype(BF16)
        b = _dot(tri, hi) + _dot(tri, lo)
        s_old = [s_ref[h] for h in range(GLA_HEADS)]
        s_new = []
        for h in range(GLA_HEADS):
            ks = slice(h * dk, (h + 1) * dk)
            vs = slice(h * dv, (h + 1) * dv)
            bh = b[:, ks]
            bl = bh[c - 1:c, :]
            qh = src[rows, h * dk:(h + 1) * dk] * (dk ** -0.5)
            kh = src[rows, nq + h * dk:nq + (h + 1) * dk]
            vh = src[rows, 2 * nq + h * dv:2 * nq + (h + 1) * dv]
            gh = src[rows, 2 * nq + nv + h * dv:2 * nq + nv + (h + 1) * dv]
            vb = vh.astype(BF16)
            qd = (qh * jnp.exp(bh)).astype(BF16)
            ki = (kh * jnp.exp(-bh)).astype(BF16)
            kd = (kh * jnp.exp(bl - bh)).astype(BF16)
            att = jnp.where(causal, _dot_nt(qd, ki), 0.0).astype(BF16)
            s = s_old[h]
            o = _dot(att, vb) + _dot(qd, s.astype(BF16))
            dl = _dot_tn(hi[:, ks], ones) + _dot_tn(lo[:, ks], ones)
            s_new.append(jnp.exp(dl) * s + _dot_tn(kd, vb))
            on = o * lax.rsqrt(jnp.mean(o * o, axis=-1, keepdims=True) + EPS) * on_ref[...]
            res = on * _silu(gh)
            if n_valid < c:
                o_ref[:, vs] = res[:n_valid]
            else:
                o_ref[rows, vs] = res
        for h in range(GLA_HEADS):
            s_ref[h] = s_new[h]

    @pl.when(li == nl - 1)
    def _():
        sfin_ref[...] = s_ref[...]


def _gla(p, n_batch, seq, w_gate2_p, b_gate, onorm_g, s0):
    n_in = p.shape[1]
    nv = GLA_HEADS * 256
    c_ref = math.gcd(seq, GLA_CHUNK)
    if seq >= GLA_CHUNK:
        chunk, rows, n_valid = c_ref, math.gcd(seq, 256), c_ref
        scratch = []
    else:
        chunk, rows, n_valid = GLA_CHUNK, seq, seq
        scratch = [pltpu.VMEM((chunk, n_in), F32)]
    n_l = seq // rows
    p3 = p.reshape(n_batch, seq, n_in)
    has_s0 = s0 is not None
    st_spec = pl.BlockSpec((None, GLA_HEADS, 128, 256), lambda b, l: (b, 0, 0, 0))
    in_specs = [
        pl.BlockSpec((None, rows, n_in), lambda b, l: (b, l, 0)),
        pl.BlockSpec((LANES, GLA_HEADS * 128), lambda b, l: (0, 0)),
        pl.BlockSpec((1, GLA_HEADS * 128), lambda b, l: (0, 0)),
        pl.BlockSpec((1, 256), lambda b, l: (0, 0)),
    ]
    args = [p3, w_gate2_p, b_gate.reshape(1, -1), onorm_g.reshape(1, -1)]
    if has_s0:
        in_specs.append(st_spec)
        args.append(s0)
    o, s_fin = pl.pallas_call(
        functools.partial(_gla_kernel, chunk=chunk, n_chunk=max(rows // chunk, 1), n_valid=n_valid,
                          has_s0=has_s0),
        grid=(n_batch, n_l),
        in_specs=in_specs,
        out_specs=[pl.BlockSpec((None, rows, nv), lambda b, l: (b, l, 0)), st_spec],
        out_shape=[jax.ShapeDtypeStruct((n_batch, seq, nv), F32),
                   jax.ShapeDtypeStruct((n_batch, GLA_HEADS, 128, 256), F32)],
        scratch_shapes=[pltpu.VMEM((GLA_HEADS, 128, 256), F32)] + scratch,
        compiler_params=_cp("parallel", "arbitrary"),
        name="gla",
    )(*args)
    return o.reshape(n_batch * seq, nv), s_fin


def _banded_kernel(q_ref, kc_ref, kp_ref, vc_ref, vp_ref, *rest, hk, grp, dil, window, has_sink, want_lse):
    rest = list(rest)
    sink_ref = rest.pop(0) if has_sink else None
    o_ref = rest.pop(0)
    lse_ref = rest.pop(0) if want_lse else None
    n = pl.program_id(1)
    i = lax.broadcasted_iota(jnp.int32, (2 * BLK, BLK), 0) & (BLK - 1)
    j = lax.broadcasted_iota(jnp.int32, (2 * BLK, BLK), 1)
    valid_c = (j <= i) & (i - j <= window)
    valid_p = i - j + BLK + jnp.where(n > 0, 0, 2 * BLK) <= window
    lo = lax.broadcasted_iota(jnp.int32, (BLK, LANES), 1) < HEAD_DIM
    top = lax.broadcasted_iota(jnp.int32, (2 * BLK, 1), 0) < BLK
    n_tile = q_ref.shape[1] // LANES
    for r in range(dil):
        rows = pl.ds(r, BLK, stride=dil) if dil > 1 else slice(None)
        for t in range(n_tile):
            kv_head = 2 * t // grp
            cols = slice((kv_head // 2) * LANES, (kv_head // 2 + 1) * LANES)
            kv = [ref[rows, cols] for ref in (kc_ref, kp_ref, vc_ref, vp_ref)]
            if grp > 1:
                own_lo = kv_head % 2 == 0
                kv = [jnp.where(lo, x if own_lo else pltpu.roll(x, HEAD_DIM, axis=1),
                                pltpu.roll(x, HEAD_DIM, axis=1) if own_lo else x) for x in kv]
            kc, kp, vc, vp = [x.astype(BF16) for x in kv]
            q2 = q_ref[rows, t * LANES:(t + 1) * LANES]
            qbd = jnp.concatenate([jnp.where(lo, q2, 0.0), jnp.where(lo, 0.0, q2)], axis=0).astype(BF16)
            sc = jnp.where(valid_c, _dot_nt(qbd, kc), NEG)
            sp = jnp.where(valid_p, _dot_nt(qbd, kp), NEG)
            m = jnp.max(jnp.maximum(sc, sp), axis=-1, keepdims=True)
            if has_sink:
                sk = jnp.where(top, sink_ref[:, 2 * t:2 * t + 1], sink_ref[:, 2 * t + 1:2 * t + 2])
                m = jnp.maximum(m, sk)
            pc = jnp.exp(sc - m)
            pp = jnp.exp(sp - m)
            l = jnp.sum(pc + pp, axis=-1, keepdims=True)
            if has_sink:
                l = l + jnp.exp(sk - m)
            pv = (_dot(pc.astype(BF16), vc) + _dot(pp.astype(BF16), vp)) / l
            o_ref[rows, t * LANES:(t + 1) * LANES] = jnp.where(lo, pv[:BLK], pv[BLK:])
            if want_lse:
                lse = m + jnp.log(l)
                lse_ref[rows, t * LANES:(t + 1) * LANES] = jnp.where(
                    lo, jnp.broadcast_to(lse[:BLK], (BLK, LANES)), jnp.broadcast_to(lse[BLK:], (BLK, LANES)))


def _banded(p, n_batch, seq, dil, q_blk, k_blk, v_blk, hk, grp, window, sinks=None, want_lse=False):
    n_in = p.shape[1]
    rows = dil * BLK
    assert seq % rows == 0
    nb = seq // rows
    p3 = p.reshape(n_batch, seq, n_in)
    qw, qi = q_blk
    kw, ki = k_blk
    vw, vi = v_blk
    if dil > 1:
        assert grp == 1 and qw == kw == vw and qw % LANES == 0
        n_cb = qw // LANES
        qi, ki, vi = qi * n_cb, ki * n_cb, vi * n_cb
        qw = kw = vw = LANES
        hk = LANES // HEAD_DIM
    else:
        n_cb = 1

    def cur(w, ci):
        return pl.BlockSpec((None, rows, w), lambda b, n, c: (b, n, ci + c))

    def prev(w, ci):
        return pl.BlockSpec((None, rows, w), lambda b, n, c: (b, jnp.maximum(n - 1, 0), ci + c))

    in_specs = [cur(qw, qi), cur(kw, ki), prev(kw, ki), cur(vw, vi), prev(vw, vi)]
    args = [p3] * 5
    if sinks is not None:
        in_specs.append(pl.BlockSpec((1, sinks.shape[0]), lambda b, n, c: (0, 0)))
        args.append(sinks.reshape(1, -1).astype(F32))
    o_spec = pl.BlockSpec((None, rows, qw), lambda b, n, c: (b, n, c))
    o_shape = jax.ShapeDtypeStruct((n_batch, seq, n_cb * qw), F32)
    outs = pl.pallas_call(
        functools.partial(_banded_kernel, hk=hk, grp=grp, dil=dil, window=window,
                          has_sink=sinks is not None, want_lse=want_lse),
        grid=(n_batch, nb, n_cb),
        in_specs=in_specs,
        out_specs=[o_spec, o_spec] if want_lse else [o_spec],
        out_shape=[o_shape, o_shape] if want_lse else [o_shape],
        compiler_params=_cp("parallel", "arbitrary", "arbitrary"),
        name="banded_attn",
    )(*args)
    return [t.reshape(n_batch * seq, n_cb * qw) for t in outs]


def _kv_tail_kernel(k_ref, v_ref, o_ref):
    w = k_ref.shape[1]
    o_ref[:w, :] = k_ref[...].T
    o_ref[w:, :] = v_ref[...].T


def _kv_tail(p, n_batch, seq, c0, hk, keep):
    n_in = p.shape[1]
    w = hk * HEAD_DIM
    rows = math.gcd(keep, 512)
    assert c0 % w == 0 and rows % LANES == 0 and (seq - keep) % rows == 0
    r0 = (seq - keep) // rows
    p3 = p.reshape(n_batch, seq, n_in)
    t = pl.pallas_call(
        _kv_tail_kernel,
        grid=(n_batch, keep // rows),
        in_specs=[pl.BlockSpec((None, rows, w), lambda b, i: (b, r0 + i, c0 // w)),
                  pl.BlockSpec((None, rows, w), lambda b, i: (b, r0 + i, c0 // w + 1))],
        out_specs=pl.BlockSpec((None, 2 * w, rows), lambda b, i: (b, 0, i)),
        out_shape=jax.ShapeDtypeStruct((n_batch, 2 * w, keep), F32),
        compiler_params=_cp("parallel", "parallel"),
        name="kv_tail",
    )(p3, p3)
    return jnp.transpose(t.reshape(n_batch, 2, hk, HEAD_DIM, keep), (0, 4, 1, 2, 3))


def _win_decode_kernel(q_ref, kn_ref, vn_ref, c_ref, *rest, bb, hk, grp, dil, win, has_sink, want_lse):
    rest = list(rest)
    sink_ref = rest.pop(0) if has_sink else None
    o_ref = rest.pop(0)
    lse_ref = rest.pop(0) if want_lse else None
    nc_ref, pad_ref = rest
    s_len = q_ref.shape[1]
    lb = c_ref.shape[2]
    dh = HEAD_DIM
    kvw = hk * dh
    n_row = grp * s_len
    new0 = LANES - s_len
    s_c = lax.broadcasted_iota(jnp.int32, (n_row, lb), 0) % s_len
    dist_c = lb + s_c - lax.broadcasted_iota(jnp.int32, (n_row, lb), 1)
    valid_c = (dist_c <= win) & ((dist_c & (dil - 1)) == 0)
    s_n = lax.broadcasted_iota(jnp.int32, (n_row, LANES), 0) % s_len
    c_n = lax.broadcasted_iota(jnp.int32, (n_row, LANES), 1) - new0
    dist_n = s_n - c_n
    valid_n = (c_n >= 0) & (dist_n >= 0) & (dist_n <= win) & ((dist_n & (dil - 1)) == 0)
    is_new = lax.broadcasted_iota(jnp.int32, (dh, LANES), 1) >= new0
    pad_ref[:, :new0, :] = jnp.zeros((bb, new0, 2 * kvw), F32)
    for b in range(bb):
        pad_ref[b, new0:, :kvw] = kn_ref[b]
        pad_ref[b, new0:, kvw:] = vn_ref[b]
        new_t = pad_ref[b].T
        for h in range(hk):
            k_rows = slice(h * dh, (h + 1) * dh)
            v_rows = slice(kvw + h * dh, kvw + (h + 1) * dh)
            k_t = c_ref[b, k_rows, :]
            v_t = c_ref[b, v_rows, :]
            kn_t = new_t[k_rows]
            vn_t = new_t[v_rows]
            q = jnp.concatenate([q_ref[b, :, (h * grp + g) * dh:(h * grp + g + 1) * dh] for g in range(grp)],
                                axis=0)
            sc = jnp.where(valid_c, _dot(q.astype(BF16), k_t.astype(BF16)), NEG)
            sn = jnp.where(valid_n, _dot(q, kn_t), NEG)
            m = jnp.maximum(jnp.max(sc, axis=-1, keepdims=True), jnp.max(sn, axis=-1, keepdims=True))
            if has_sink:
                sk = jnp.concatenate(
                    [jnp.broadcast_to(sink_ref[:, h * grp + g:h * grp + g + 1], (s_len, 1)) for g in range(grp)],
                    axis=0)
                m = jnp.maximum(m, sk)
            pc = jnp.exp(sc - m)
            pn = jnp.exp(sn - m)
            l = jnp.sum(pc, axis=-1, keepdims=True) + jnp.sum(pn, axis=-1, keepdims=True)
            if has_sink:
                l = l + jnp.exp(sk - m)
            o = (_dot_nt(pc.astype(BF16), v_t.astype(BF16)) + _dot_nt(pn, vn_t)) / l
            for g in range(grp):
                qs = slice((h * grp + g) * dh, (h * grp + g + 1) * dh)
                o_ref[b, :, qs] = o[g * s_len:(g + 1) * s_len]
                if want_lse:
                    lse_ref[b, :, qs] = jnp.broadcast_to((m + jnp.log(l))[g * s_len:(g + 1) * s_len], (s_len, dh))
            for rows, x_t, x_new in ((k_rows, k_t, kn_t), (v_rows, v_t, vn_t)):
                moved = pltpu.roll(x_t, lb - s_len, axis=1)
                nc_ref[b, rows, :] = moved
                nc_ref[b, rows, lb - LANES:] = jnp.where(is_new, x_new, moved[:, lb - LANES:])


def _win_decode(p, n_batch, s_len, buf, q_blk, k_blk, v_blk, hk, grp, dil, win, sinks=None, want_lse=False):
    n_in = p.shape[1]
    lb = buf.shape[1]
    assert lb == win and lb % LANES == 0 and dil & (dil - 1) == 0 and s_len <= LANES
    dh = HEAD_DIM
    qw, kvw = hk * grp * dh, hk * dh
    buf_t = jnp.transpose(buf, (0, 2, 3, 4, 1)).reshape(n_batch, 2 * kvw, lb)
    bb = max(1, min(math.gcd(n_batch, 8), (2 * 1024 * 1024) // (lb * 2 * kvw * 4)))
    p3 = p.reshape(n_batch, s_len, n_in)
    (qwid, qi), (kwid, ki), (vwid, vi) = q_blk, k_blk, v_blk
    assert qwid == qw and kwid == kvw and vwid == kvw
    q_spec = pl.BlockSpec((bb, s_len, qw), lambda i: (i, 0, 0))
    buf_spec = pl.BlockSpec((bb, 2 * kvw, lb), lambda i: (i, 0, 0))
    in_specs = [
        pl.BlockSpec((bb, s_len, qw), lambda i: (i, 0, qi)),
        pl.BlockSpec((bb, s_len, kvw), lambda i: (i, 0, ki)),
        pl.BlockSpec((bb, s_len, kvw), lambda i: (i, 0, vi)),
        buf_spec,
    ]
    args = [p3, p3, p3, buf_t]
    if sinks is not None:
        in_specs.append(pl.BlockSpec((1, hk * grp), lambda i: (0, 0)))
        args.append(sinks.reshape(1, -1).astype(F32))
    q_shape = jax.ShapeDtypeStruct((n_batch, s_len, qw), F32)
    outs = pl.pallas_call(
        functools.partial(_win_decode_kernel, bb=bb, hk=hk, grp=grp, dil=dil, win=win,
                          has_sink=sinks is not None, want_lse=want_lse),
        grid=(n_batch // bb,),
        in_specs=in_specs,
        out_specs=[q_spec] * (2 if want_lse else 1) + [buf_spec],
        out_shape=[q_shape] * (2 if want_lse else 1) + [jax.ShapeDtypeStruct(buf_t.shape, F32)],
        scratch_shapes=[pltpu.VMEM((bb, LANES, 2 * kvw), F32)],
        compiler_params=_cp("parallel"),
        name="win_decode",
    )(*args)
    nbuf = jnp.transpose(outs[-1].reshape(n_batch, 2, hk, dh, lb), (0, 4, 1, 2, 3))
    return [t.reshape(n_batch * s_len, qw) for t in outs[:-1]] + [nbuf]


def _diff_lambda(lam_ref):
    lam = lam_ref[...]
    a = jnp.sum(lam[0:1] * lam[1:2], axis=-1, keepdims=True)
    b = jnp.sum(lam[2:3] * lam[3:4], axis=-1, keepdims=True)
    return jnp.exp(a) - jnp.exp(b) + LAMBDA_INIT


def _diff_head_out(a1, a2, lam, sub_g):
    o = a1 - lam * a2
    return o * lax.rsqrt(jnp.mean(o * o, axis=-1, keepdims=True) + EPS) * sub_g * (1.0 - LAMBDA_INIT)


def _diff_prompt_kernel(q_ref, kv_ref, lam_ref, sub_ref, o_ref, qb_ref, m_ref, l_ref, acc_ref, *, tq):
    qi = pl.program_id(1)
    dh = HEAD_DIM
    dv = 2 * dh
    kw = DIFF_KV_HEADS * dv
    rows2 = DIFF_GROUP * tq
    r_i = lax.broadcasted_iota(jnp.int32, (rows2, tq), 0) % tq
    c_i = lax.broadcasted_iota(jnp.int32, (rows2, tq), 1)
    diag = c_i <= r_i
    lam = _diff_lambda(lam_ref)
    for h in range(DIFF_KV_HEADS):
        for mp in range(2):
            qb_ref[h * 2 + mp] = jnp.concatenate(
                [q_ref[:, ((h * DIFF_GROUP + g) * 2 + mp) * dh:((h * DIFF_GROUP + g) * 2 + mp + 1) * dh]
                 for g in range(DIFF_GROUP)], axis=0).astype(BF16)
    m_ref[...] = jnp.full_like(m_ref, NEG)
    l_ref[...] = jnp.zeros_like(l_ref)
    acc_ref[...] = jnp.zeros_like(acc_ref)

    def step(kb, carry, masked):
        r0 = pl.multiple_of(kb * tq, tq)
        n_map = DIFF_KV_HEADS * 2
        old = [(m_ref[c], l_ref[c], acc_ref[c]) for c in range(n_map)]
        new = []
        for h in range(DIFF_KV_HEADS):
            v = kv_ref[pl.ds(r0, tq), kw + h * dv:kw + (h + 1) * dv].astype(BF16)
            for mp in range(2):
                m_old, l_old, acc_old = old[h * 2 + mp]
                k = kv_ref[pl.ds(r0, tq), h * dv + mp * dh:h * dv + (mp + 1) * dh].astype(BF16)
                s = _dot_nt(qb_ref[h * 2 + mp], k)
                if masked:
                    s = jnp.where(diag, s, NEG)
                parts = [s[:, i * LANES:(i + 1) * LANES] for i in range(tq // LANES)]
                mx = parts[0]
                for t in parts[1:]:
                    mx = jnp.maximum(mx, t)
                m_new = jnp.maximum(m_old, jnp.max(mx, axis=-1, keepdims=True))
                corr = jnp.exp(m_old - m_new)
                ps = [jnp.exp(t - m_new) for t in parts]
                l_new = l_old * corr
                for t in ps:
                    l_new = l_new + t
                p = jnp.concatenate(ps, axis=1).astype(BF16)
                new.append((m_new, l_new, acc_old * corr + _dot(p, v)))
        for c in range(n_map):
            m_ref[c], l_ref[c], acc_ref[c] = new[c]
        return carry

    lax.fori_loop(0, qi, functools.partial(step, masked=False), 0)
    step(qi, 0, True)
    for h in range(DIFF_KV_HEADS):
        res = [acc_ref[h * 2 + mp] / jnp.sum(l_ref[h * 2 + mp], axis=-1, keepdims=True) for mp in range(2)]
        for g in range(DIFF_GROUP):
            hq = h * DIFF_GROUP + g
            o_ref[:, hq * dv:(hq + 1) * dv] = _diff_head_out(res[0][g * tq:(g + 1) * tq], res[1][g * tq:(g + 1) * tq],
                                                             lam, sub_ref[...])


def _diff_prompt(p, n_batch, seq, lam_vec, subln_g):
    n_in = p.shape[1]
    nq = DIFF_KV_HEADS * DIFF_GROUP * 2 * HEAD_DIM
    tq = math.gcd(seq, 256)
    n_map = DIFF_KV_HEADS * 2
    assert 2 * HEAD_DIM == LANES and tq % LANES == 0
    p3 = p.reshape(n_batch, seq, n_in)
    o = pl.pallas_call(
        functools.partial(_diff_prompt_kernel, tq=tq),
        grid=(n_batch, seq // tq),
        in_specs=[
            pl.BlockSpec((None, tq, nq), lambda b, i: (b, i, 0)),
            pl.BlockSpec((None, seq, nq), lambda b, i: (b, 0, 1)),
            pl.BlockSpec((4, HEAD_DIM), lambda b, i: (0, 0)),
            pl.BlockSpec((1, 2 * HEAD_DIM), lambda b, i: (0, 0)),
        ],
        out_specs=pl.BlockSpec((None, tq, nq), lambda b, i: (b, i, 0)),
        out_shape=jax.ShapeDtypeStruct((n_batch, seq, nq), F32),
        scratch_shapes=[pltpu.VMEM((n_map, DIFF_GROUP * tq, HEAD_DIM), BF16)]
        + [pltpu.VMEM((n_map, DIFF_GROUP * tq, LANES), F32)] * 3,
        compiler_params=_cp("parallel", "arbitrary"),
        name="diff_attn",
    )(p3, p3, lam_vec.astype(F32), subln_g.reshape(1, -1).astype(F32))
    return o.reshape(n_batch * seq, nq)


def _diff_sample_kernel(pt_ref, q_ref, kv_ref, lam_ref, sub_ref, *rest, pps):
    pages = rest[:pps]
    o_ref, qb_ref, m_ref, l_ref, acc_ref, nk_ref = rest[pps:]
    step = pl.program_id(1)
    n_step = pl.num_programs(1)
    s_len = q_ref.shape[0]
    dh = HEAD_DIM
    dv = 2 * dh
    kw = DIFF_KV_HEADS * dv
    hr = DIFF_GROUP * 2 * s_len
    stride = 2 * DIFF_KV_HEADS

    @pl.when(step == 0)
    def _():
        qb_ref[...] = jnp.zeros_like(qb_ref)
        for idx in range(DIFF_KV_HEADS * DIFF_GROUP * 2):
            mp = idx % 2
            qb_ref[idx * s_len:(idx + 1) * s_len, mp * dh:(mp + 1) * dh] = q_ref[:, idx * dh:(idx + 1) * dh]
        m_ref[...] = jnp.full_like(m_ref, NEG)
        l_ref[...] = jnp.zeros_like(l_ref)
        acc_ref[...] = jnp.zeros_like(acc_ref)

    def update(parts_h, v_h):
        heads = range(DIFF_KV_HEADS)
        old = [(m_ref[h * hr:(h + 1) * hr], l_ref[h * hr:(h + 1) * hr], acc_ref[h * hr:(h + 1) * hr]) for h in heads]
        new = []
        for h in heads:
            m_old, l_old, acc_old = old[h]
            parts = parts_h[h]
            mx = parts[0]
            for t in parts[1:]:
                mx = jnp.maximum(mx, t)
            m_new = jnp.maximum(m_old, jnp.max(mx, axis=-1, keepdims=True))
            corr = jnp.exp(m_old - m_new)
            ps = [jnp.exp(t - m_new) for t in parts]
            l_new = l_old * corr
            for t in ps:
                l_new = l_new + t
            p = ps[0] if len(ps) == 1 else jnp.concatenate(ps, axis=1)
            new.append((m_new, l_new, acc_old * corr + _dot(p.astype(BF16), v_h[h])))
        for h in heads:
            m_ref[h * hr:(h + 1) * hr], l_ref[h * hr:(h + 1) * hr], acc_ref[h * hr:(h + 1) * hr] = new[h]

    qhs = [qb_ref[h * hr:(h + 1) * hr, :].astype(BF16) for h in range(DIFF_KV_HEADS)]
    update([[_dot_nt(qhs[h], pg[pl.ds(h, PAGE_SIZE, stride=stride), :].astype(BF16)) for pg in pages]
            for h in range(DIFF_KV_HEADS)],
           [jnp.concatenate([pg[pl.ds(DIFF_KV_HEADS + h, PAGE_SIZE, stride=stride), :] for pg in pages],
                            axis=0).astype(BF16) for h in range(DIFF_KV_HEADS)])

    @pl.when(step == n_step - 1)
    def _():
        nk_ref[...] = jnp.zeros_like(nk_ref)
        nk_ref[:s_len, :] = kv_ref[...]
        r_s = lax.broadcasted_iota(jnp.int32, (hr, PAGE_SIZE), 0) % s_len
        c_s = lax.broadcasted_iota(jnp.int32, (hr, PAGE_SIZE), 1)
        update([[jnp.where(c_s <= r_s, _dot_nt(qhs[h], nk_ref[:, h * dv:(h + 1) * dv].astype(BF16)), NEG)]
                for h in range(DIFF_KV_HEADS)],
               [nk_ref[:, kw + h * dv:kw + (h + 1) * dv].astype(BF16) for h in range(DIFF_KV_HEADS)])
        lam = _diff_lambda(lam_ref)
        res = acc_ref[...] / jnp.sum(l_ref[...], axis=-1, keepdims=True)
        for hq in range(DIFF_KV_HEADS * DIFF_GROUP):
            a1 = res[(hq * 2) * s_len:(hq * 2 + 1) * s_len]
            a2 = res[(hq * 2 + 1) * s_len:(hq * 2 + 2) * s_len]
            o_ref[:, hq * dv:(hq + 1) * dv] = _diff_head_out(a1, a2, lam, sub_ref[...])


def _diff_sample(p, n_batch, s_len, cache, page_table, lam_vec, subln_g):
    n_in = p.shape[1]
    nq = DIFF_KV_HEADS * DIFF_GROUP * 2 * HEAD_DIM
    n_pages = page_table.shape[1]
    pps = math.gcd(n_pages, 32)
    n_rows = DIFF_KV_HEADS * DIFF_GROUP * 2 * s_len
    p3 = p.reshape(n_batch, s_len, n_in)
    kvw = 2 * DIFF_KV_HEADS * 2 * HEAD_DIM
    assert 2 * HEAD_DIM == LANES and PAGE_SIZE == LANES
    page_rows = PAGE_SIZE * 2 * DIFF_KV_HEADS
    cache3 = cache.reshape(cache.shape[0] * page_rows, LANES)

    def page_spec(jj):
        return pl.BlockSpec((page_rows, LANES), lambda b, s, pt: (pt[b * n_pages + s * pps + jj], 0))

    grid_spec = pltpu.PrefetchScalarGridSpec(
        num_scalar_prefetch=1,
        grid=(n_batch, n_pages // pps),
        in_specs=[
            pl.BlockSpec((None, s_len, nq), lambda b, s, pt: (b, 0, 0)),
            pl.BlockSpec((None, s_len, nq), lambda b, s, pt: (b, 0, 1)),
            pl.BlockSpec((4, HEAD_DIM), lambda b, s, pt: (0, 0)),
            pl.BlockSpec((1, 2 * HEAD_DIM), lambda b, s, pt: (0, 0)),
        ] + [page_spec(jj) for jj in range(pps)],
        out_specs=pl.BlockSpec((None, s_len, nq), lambda b, s, pt: (b, 0, 0)),
        scratch_shapes=[pltpu.VMEM((n_rows, LANES), F32)] * 4 + [pltpu.VMEM((PAGE_SIZE, kvw), F32)],
    )
    o = pl.pallas_call(
        functools.partial(_diff_sample_kernel, pps=pps),
        grid_spec=grid_spec,
        out_shape=jax.ShapeDtypeStruct((n_batch, s_len, nq), F32),
        compiler_params=_cp("parallel", "arbitrary"),
        name="diff_decode",
    )(page_table.reshape(-1), p3, p3, lam_vec.astype(F32), subln_g.reshape(1, -1).astype(F32),
      *([cache3] * pps))
    return o.reshape(n_batch * s_len, nq)


def kernel(x_prompt, x_sample, state_gla, cache_swa, cache_dil1, cache_dil2, cache_dil3, cache_diff, page_table, c_prompt, c_sample, norm1_g, norm2_g, ada_w, ada_b, gla_w_in, gla_w_gate2, gla_b_gate, gla_onorm_g, gla_w_out, swa_w_in, swa_q_norm, swa_k_norm, swa_sinks, swa_w_out, dil_w_in, dil_q_norm, dil_k_norm, dil_w_out, diff_w_in, diff_q_norm, diff_k_norm, diff_lambda, diff_subln_g, diff_w_out, ffn_w_gu, ffn_w_down, moe_router, moe_w_gu, moe_w_down):
    bp, seq, d = x_prompt.shape
    db, s_len, _ = x_sample.shape
    depth = ada_w.shape[0]
    dh = HEAD_DIM
    qscale = dh ** -0.5

    mods = _ada(jnp.concatenate([c_prompt, c_sample], axis=0), ada_w, ada_b)
    xp = x_prompt.reshape(bp * seq, d)
    xs = x_sample.reshape(db * s_len, d)
    out = {}

    for i in range(depth):
        gp = _Group(bp, seq, mods[i, :bp])
        gs = _Group(db, s_len, mods[i, bp:])
        kind = i % 4
        if kind == 0:
            n_in = gla_w_in.shape[1]
            n_pad = -(-n_in // (5 * LANES)) * (5 * LANES)
            w_in = jnp.pad(gla_w_in, ((0, 0), (0, n_pad - n_in)))
            lr_w = n_pad - (n_in - GLA_RANK)
            wg2 = jnp.pad(gla_w_gate2, ((0, lr_w - GLA_RANK), (0, 0)))
            pp = _inproj(gp, xp, norm1_g[i], 0, w_in, n_pad // 5)
            ps = _inproj(gs, xs, norm1_g[i], 0, w_in, n_pad // 5)
            op, out["gla_p"] = _gla(pp, bp, seq, wg2, gla_b_gate, gla_onorm_g, None)
            os_, out["gla_s"] = _gla(ps, db, s_len, wg2, gla_b_gate, gla_onorm_g, state_gla)
            xp = _outproj(gp, xp, op, gla_w_out, 2)
            xs = _outproj(gs, xs, os_, gla_w_out, 2)
        elif kind == 1:
            hk = SWA_KV_HEADS
            nq = swa_w_out.shape[0]
            grp = nq // dh // hk
            kvw = hk * dh
            norm = _norm_rows([(nq, swa_q_norm, qscale), (kvw, swa_k_norm, 1.0), (kvw, None, 1.0)])
            pp = _inproj(gp, xp, norm1_g[i], 0, swa_w_in, kvw * 2, norm)
            ps = _inproj(gs, xs, norm1_g[i], 0, swa_w_in, kvw * 2, norm)
            (op,) = _banded(pp, bp, seq, 1, (nq, 0), (kvw, nq // kvw), (kvw, nq // kvw + 1),
                            hk, grp, SWA_WINDOW, sinks=swa_sinks)
            keep = min(SWA_WINDOW, seq)
            out["swa_p"] = _kv_tail(pp, bp, seq, nq, hk, keep)
            os_, out["swa_s"] = _win_decode(ps, db, s_len, cache_swa, (nq, 0), (kvw, nq // kvw),
                                            (kvw, nq // kvw + 1), hk, grp, 1, SWA_WINDOW, sinks=swa_sinks)
            xp = _outproj(gp, xp, op, swa_w_out, 2)
            xs = _outproj(gs, xs, os_, swa_w_out, 2)
        elif kind == 2:
            w = DIL_HEADS * dh
            norm = _norm_rows([(w, dil_q_norm, qscale), (w, dil_k_norm, 1.0), (w, None, 1.0)] * len(DIL_GROUPS))
            pp = _inproj(gp, xp, norm1_g[i], 0, dil_w_in, w, norm)
            ps = _inproj(gs, xs, norm1_g[i], 0, dil_w_in, w, norm)
            outs_p, lses_p, outs_s, lses_s = [], [], [], []
            caches = (cache_dil1, cache_dil2, cache_dil3)
            for gi, (win, dil) in enumerate(DIL_GROUPS):
                o, lse = _banded(pp, bp, seq, dil, (w, 3 * gi), (w, 3 * gi + 1), (w, 3 * gi + 2),
                                 DIL_HEADS, 1, win // dil, want_lse=True)
                outs_p.append(o)
                lses_p.append(lse)
                keep = min(win, seq)
                out["dil%d_p" % gi] = _kv_tail(pp, bp, seq, (3 * gi + 1) * w, DIL_HEADS, keep)
                o, lse, out["dil%d_s" % gi] = _win_decode(ps, db, s_len, caches[gi], (w, 3 * gi), (w, 3 * gi + 1),
                                                          (w, 3 * gi + 2), DIL_HEADS, 1, dil, win, want_lse=True)
                outs_s.append(o)
                lses_s.append(lse)
            xp = _outproj_mix(gp, xp, outs_p, lses_p, dil_w_out, 2)
            xs = _outproj_mix(gs, xs, outs_s, lses_s, dil_w_out, 2)
        else:
            nq = DIFF_KV_HEADS * DIFF_GROUP * 2 * dh
            nk = DIFF_KV_HEADS * 2 * dh
            norm = _norm_rows([(nq, diff_q_norm, qscale), (nk, diff_k_norm, 1.0), (nk, None, 1.0)])
            pp = _inproj(gp, xp, norm1_g[i], 0, diff_w_in, nk, norm)
            ps = _inproj(gs, xs, norm1_g[i], 0, diff_w_in, nk, norm)
            op = _diff_prompt(pp, bp, seq, diff_lambda, diff_subln_g)
            os_ = _diff_sample(ps, db, s_len, cache_diff, page_table, diff_lambda, diff_subln_g)
            out["diff_p"] = pp[:, nq:].reshape(bp, seq, 2, DIFF_KV_HEADS, 2 * dh)
            out["diff_s"] = ps[:, nq:].reshape(db, s_len, 2, DIFF_KV_HEADS, 2 * dh)
            xp = _outproj(gp, xp, op, diff_w_out, 2)
            xs = _outproj(gs, xs, os_, diff_w_out, 2)
        j = i // 2
        if i % 2 == 0:
            w_gu, w_down = ffn_w_gu[j].astype(BF16), ffn_w_down[j].astype(BF16)
            xp = _ffn(gp, xp, norm2_g[i], w_gu, w_down, 1408)
            xs = _ffn(gs, xs, norm2_g[i], w_gu, w_down, 1408)
        else:
            xp = _moe(gp, xp, norm2_g[i], moe_router[j], moe_w_gu, moe_w_down, j, 512)
            xs = _moe(gs, xs, norm2_g[i], moe_router[j], moe_w_gu, moe_w_down, j, 512)

    return (xp.reshape(bp, seq, d), xs.reshape(db, s_len, d), out["gla_p"], out["gla_s"],
            out["swa_p"], out["swa_s"], out["dil0_p"], out["dil0_s"], out["dil1_p"], out["dil1_s"],
            out["dil2_p"], out["dil2_s"], out["diff_p"], out["diff_s"])
```
